```python
import math
import jax, jax.numpy as jnp
from jax import lax
import numpy as np

D_MODEL = 1024
BATCH = 8
SEQ = 2048
DEPTH = 2

GRID_W = 64
CTX_LEN = 256
EPS = 1e-6

GDN_HEADS = 8
GDN_DK = 64
GDN_DV = 64
GDN_KEY_W = GDN_HEADS * GDN_DK
GDN_VAL_W = GDN_HEADS * GDN_DV
CONV_W = 2 * GDN_KEY_W + GDN_VAL_W
CONV_K = 5
CHUNK = 64

MLA_HEADS = 8
QK_NOPE = 64
QK_ROPE = 32
V_DIM = 64
Q_LORA = 256
KV_LORA = 128
ROPE_THETA = 10000.0
AXIS_DIM = QK_ROPE // 2
Q_BLOCK = 128

MIX_WIDTH = GDN_VAL_W + MLA_HEADS * V_DIM
FFN_HIDDEN = -(-8 * D_MODEL // (3 * 256)) * 256

IN_SPLITS = (CONV_W, GDN_VAL_W, 2 * GDN_HEADS, 2 * GDN_HEADS, Q_LORA, KV_LORA, QK_ROPE)
IN_COLS = sum(IN_SPLITS)

kernel_name = 'hybrid_gdn_mla_dit_block'


def rmsnorm(x, g):
    xf = x.astype(jnp.float32)
    y = xf * lax.rsqrt(jnp.mean(xf * xf, axis=-1, keepdims=True) + EPS)
    return (y * g.astype(jnp.float32)).astype(x.dtype)


def l2norm(x):
    xf = x.astype(jnp.float32)
    return xf * lax.rsqrt(jnp.sum(xf * xf, axis=-1, keepdims=True) + EPS)


def modulate(x, shift, scale):
    return x * (1 + scale) + shift


def swiglu(h, w_gate, w_up, w_down):
    return (jax.nn.silu(h @ w_gate) * (h @ w_up)) @ w_down


def split_cols(p):
    out, start = [], 0
    for size in IN_SPLITS:
        out.append(p[..., start:start + size])
        start += size
    return out


def axial_rope_tables(T, dtype):
    rows = T // GRID_W
    row = jnp.repeat(jnp.arange(rows), GRID_W).astype(jnp.float32)
    col = jnp.tile(jnp.arange(GRID_W), rows).astype(jnp.float32)
    inv_freq = ROPE_THETA ** (-jnp.arange(0, AXIS_DIM, 2, dtype=jnp.float32) / AXIS_DIM)

    def axis_angles(pos):
        ang = pos[:, None] * inv_freq[None, :]
        return jnp.concatenate([ang, ang], axis=-1)

    ang = jnp.concatenate([axis_angles(row), axis_angles(col)], axis=-1)
    return jnp.cos(ang).astype(dtype), jnp.sin(ang).astype(dtype)


def rotate_half_axial(x):
    quarter = AXIS_DIM // 2

    def rh(t):
        return jnp.concatenate([-t[..., quarter:], t[..., :quarter]], axis=-1)

    return jnp.concatenate([rh(x[..., :AXIS_DIM]), rh(x[..., AXIS_DIM:])], axis=-1)


def apply_rope(x, cos, sin):
    return x * cos + rotate_half_axial(x) * sin


def gdn_qkv(qkv, conv_w):
    B, T, _ = qkv.shape
    u = lax.conv_general_dilated(qkv, conv_w[:, None, :].astype(qkv.dtype), window_strides=(1,),
                                 padding=((CONV_K // 2, CONV_K // 2),),
                                 dimension_numbers=('NWC', 'WIO', 'NWC'), feature_group_count=CONV_W)
    u = jax.nn.silu(u)
    q = l2norm(u[..., :GDN_KEY_W].reshape(B, T, GDN_HEADS, GDN_DK)) * (GDN_DK ** -0.5)
    k = l2norm(u[..., GDN_KEY_W:2 * GDN_KEY_W].reshape(B, T, GDN_HEADS, GDN_DK))
    v = u[..., 2 * GDN_KEY_W:].reshape(B, T, GDN_HEADS, GDN_DV)
    return q, k, v


def gdn_gates(a, b, a_log, dt_bias, d):
    sl = slice(d * GDN_HEADS, (d + 1) * GDN_HEADS)
    g = -jnp.exp(a_log[d].astype(jnp.float32)) * jax.nn.softplus(
        a[..., sl].astype(jnp.float32) + dt_bias[d].astype(jnp.float32))
    beta = jax.nn.sigmoid(b[..., sl].astype(jnp.float32))
    return g, beta


def gdn_chunked(q, k, v, g, beta, s0):
    B, T, H, DK = q.shape
    DV = v.shape[-1]
    n = T // CHUNK

    def chunks(t):
        t = jnp.moveaxis(t.astype(jnp.float32), 2, 1)
        return t.reshape(B, H, n, CHUNK, *t.shape[3:])

    q, k, v, g, beta = chunks(q), chunks(k), chunks(v), chunks(g), chunks(beta)
    gc = jnp.cumsum(g, axis=-1)
    idx = jnp.arange(CHUNK)
    incl = idx[:, None] >= idx[None, :]
    strict = idx[:, None] > idx[None, :]
    decay = jnp.exp(jnp.where(incl, gc[..., :, None] - gc[..., None, :], -jnp.inf))
    kb = k * beta[..., None]
    a_mat = jnp.where(strict, jnp.einsum('bhnid,bhnjd->bhnij', kb, k) * decay, 0.0)
    eye = jnp.eye(CHUNK, dtype=jnp.float32)
    rhs = jnp.concatenate([v * beta[..., None], kb * jnp.exp(gc)[..., None]], axis=-1)
    sol = lax.linalg.triangular_solve(a_mat + eye, rhs, left_side=True, lower=True)
    u, w = sol[..., :DV], sol[..., DV:]
    qk = jnp.einsum('bhnid,bhnjd->bhnij', q, k) * decay
    qg = q * jnp.exp(gc)[..., None]
    kg = k * jnp.exp(gc[..., -1:] - gc)[..., None]
    g_last = jnp.exp(gc[..., -1])

    def step(S, xs):
        qk_i, qg_i, kg_i, u_i, w_i, gl_i = xs
        v_new = u_i - jnp.einsum('bhck,bhkv->bhcv', w_i, S)
        o_i = jnp.einsum('bhck,bhkv->bhcv', qg_i, S) + jnp.einsum('bhij,bhjv->bhiv', qk_i, v_new)
        S = S * gl_i[..., None, None] + jnp.einsum('bhck,bhcv->bhkv', kg_i, v_new)
        return S, o_i

    xs = tuple(jnp.moveaxis(t, 2, 0) for t in (qk, qg, kg, u, w, g_last))
    s_final, o = lax.scan(step, s0.astype(jnp.float32), xs)
    o = jnp.moveaxis(o, 0, 2).reshape(B, H, T, DV)
    return jnp.moveaxis(o, 1, 2), s_final


def gdn_bidir(q, k, v, a, b, a_log, dt_bias, s0_f, s0_b):
    g_f, beta_f = gdn_gates(a, b, a_log, dt_bias, 0)
    g_b, beta_b = gdn_gates(a, b, a_log, dt_bias, 1)
    o_f, s_f = gdn_chunked(q, k, v, g_f, beta_f, s0_f)
    flip = lambda t: jnp.flip(t, axis=1)
    o_b, s_b = gdn_chunked(flip(q), flip(k), flip(v), flip(g_b), flip(beta_b), s0_b)
    return o_f + flip(o_b), s_f, s_b


def gdn_output(o, z, gdn_norm_g):
    B, T = z.shape[:2]
    y = rmsnorm(o, gdn_norm_g).astype(z.dtype) * jax.nn.silu(z.reshape(B, T, GDN_HEADS, GDN_DV))
    return y.reshape(B, T, GDN_VAL_W)


def mla_queries(c_q, q_norm_g, w_uq, cos, sin):
    B, T, _ = c_q.shape
    q = (rmsnorm(c_q, q_norm_g) @ w_uq).reshape(B, T, MLA_HEADS, QK_NOPE + QK_ROPE)
    q_nope, q_pe = q[..., :QK_NOPE], q[..., QK_NOPE:]
    if cos is not None:
        q_pe = apply_rope(q_pe, cos[:, None, :], sin[:, None, :])
    return q_nope, q_pe


def mla_keys(c_kv, k_rope, kv_norm_g, w_ukv, cos, sin):
    B, T, _ = c_kv.shape
    kv = (rmsnorm(c_kv, kv_norm_g) @ w_ukv).reshape(B, T, MLA_HEADS, QK_NOPE + V_DIM)
    k_nope, v = kv[..., :QK_NOPE], kv[..., QK_NOPE:]
    k_pe = k_rope
    if cos is not None:
        k_pe = apply_rope(k_pe, cos, sin)
    return k_nope, k_pe, v


def attend(q_nope, q_pe, k_nope, k_pe, v):
    s = (jnp.einsum('bqhd,bkhd->bhqk', q_nope, k_nope)
         + jnp.einsum('bqhd,bkd->bhqk', q_pe, k_pe)) * ((QK_NOPE + QK_ROPE) ** -0.5)
    p = jax.nn.softmax(s.astype(jnp.float32), axis=-1).astype(v.dtype)
    return jnp.einsum('bhqk,bkhd->bqhd', p, v)


def latent_attention(q_nope, q_pe, k_nope, k_pe, v):
    B, T = q_nope.shape[:2]
    nblk = T // Q_BLOCK

    def blocks(t):
        return jnp.moveaxis(t.reshape(B, nblk, Q_BLOCK, *t.shape[2:]), 1, 0)

    o = lax.map(lambda qb: attend(qb[0], qb[1], k_nope, k_pe, v), (blocks(q_nope), blocks(q_pe)))
    return jnp.moveaxis(o, 0, 1).reshape(B, T, MLA_HEADS * V_DIM)


def hybrid_mixer(h, hc, w_in, conv_w, a_log, dt_bias, gdn_norm_g, q_norm_g, w_uq, kv_norm_g, w_ukv,
                 w_out, cos, sin, need_ctx_out):
    B, S, _ = h.shape
    qkv, z, a, b, c_q, c_kv, k_rope = split_cols(h @ w_in)
    qkv_c, z_c, a_c, b_c, c_q_c, c_kv_c, k_rope_c = split_cols(hc @ w_in)

    q, k, v = gdn_qkv(qkv, conv_w)
    qc, kc, vc = gdn_qkv(qkv_c, conv_w)
    s0 = jnp.zeros((B, GDN_HEADS, GDN_DK, GDN_DV), jnp.float32)
    o_c, s_f, s_b = gdn_bidir(qc, kc, vc, a_c, b_c, a_log, dt_bias, s0, s0)
    o, _, _ = gdn_bidir(q, k, v, a, b, a_log, dt_bias, s_f, s_b)
    y_gdn = gdn_output(o, z, gdn_norm_g)

    qn, qp = mla_queries(c_q, q_norm_g, w_uq, cos, sin)
    kn, kp, vm = mla_keys(c_kv, k_rope, kv_norm_g, w_ukv, cos, sin)
    knc, kpc, vmc = mla_keys(c_kv_c, k_rope_c, kv_norm_g, w_ukv, None, None)
    kn_all = jnp.concatenate([knc, kn], axis=1)
    kp_all = jnp.concatenate([kpc, kp], axis=1)
    v_all = jnp.concatenate([vmc, vm], axis=1)
    y_mla = latent_attention(qn, qp, kn_all, kp_all, v_all)

    y = jnp.concatenate([y_gdn, y_mla], axis=-1) @ w_out
    if need_ctx_out:
        qnc, qpc = mla_queries(c_q_c, q_norm_g, w_uq, None, None)
        yc_mla = attend(qnc, qpc, knc, kpc, vmc).reshape(B, hc.shape[1], MLA_HEADS * V_DIM)
        yc = jnp.concatenate([gdn_output(o_c, z_c, gdn_norm_g), yc_mla], axis=-1) @ w_out
    else:
        yc = None
    return y, yc


def setup_inputs(seed: int = 0) -> dict:
    key = jax.random.key(seed)
    ks = jax.random.split(key, 24)
    f32 = jnp.float32

    def nrm(k, shape, fan_in, scale=1.0):
        return jax.random.normal(k, shape, f32) * (scale * fan_in ** -0.5)

    def gain(k, shape):
        return 1.0 + 0.02 * jax.random.normal(k, shape, f32)

    dt = jnp.exp(jax.random.uniform(ks[9], (DEPTH, 2, GDN_HEADS), f32, math.log(1e-3), math.log(1e-1)))
    return {
        'x': jax.random.normal(ks[0], (BATCH, SEQ, D_MODEL), f32),
        'c': jax.random.normal(ks[1], (BATCH, D_MODEL), f32),
        'ctx': jax.random.normal(ks[2], (BATCH, CTX_LEN, D_MODEL), f32),
        'c_ctx': jax.random.normal(ks[3], (D_MODEL,), f32),
        'w_mod': nrm(ks[4], (DEPTH, D_MODEL, 6 * D_MODEL), D_MODEL, 0.5),
        'b_mod': 0.02 * jax.random.normal(ks[5], (DEPTH, 6 * D_MODEL), f32),
        'norm1_g': gain(ks[6], (DEPTH, D_MODEL)),
        'norm2_g': gain(ks[7], (DEPTH, D_MODEL)),
        'w_in': nrm(ks[8], (DEPTH, D_MODEL, IN_COLS), D_MODEL),
        'conv_w': nrm(ks[10], (DEPTH, CONV_K, CONV_W), CONV_K),
        'a_log': jnp.log(jax.random.uniform(ks[11], (DEPTH, 2, GDN_HEADS), f32, 1.0, 16.0)),
        'dt_bias': dt + jnp.log(-jnp.expm1(-dt)),
        'gdn_norm_g': gain(ks[12], (DEPTH, GDN_DV)),
        'q_norm_g': gain(ks[13], (DEPTH, Q_LORA)),
        'w_uq': nrm(ks[14], (DEPTH, Q_LORA, MLA_HEADS * (QK_NOPE + QK_ROPE)), Q_LORA),
        'kv_norm_g': gain(ks[15], (DEPTH, KV_LORA)),
        'w_ukv': nrm(ks[16], (DEPTH, KV_LORA, MLA_HEADS * (QK_NOPE + V_DIM)), KV_LORA),
        'w_out': nrm(ks[17], (DEPTH, MIX_WIDTH, D_MODEL), MIX_WIDTH),
        'w_gate': nrm(ks[18], (DEPTH, D_MODEL, FFN_HIDDEN), D_MODEL),
        'w_up': nrm(ks[19], (DEPTH, D_MODEL, FFN_HIDDEN), D_MODEL),
        'w_down': nrm(ks[20], (DEPTH, FFN_HIDDEN, D_MODEL), FFN_HIDDEN),
        'final_norm_g': gain(ks[21], (D_MODEL,)),
    }


def reference(x, c, ctx, c_ctx, w_mod, b_mod, norm1_g, norm2_g, w_in, conv_w, a_log, dt_bias, gdn_norm_g,
              q_norm_g, w_uq, kv_norm_g, w_ukv, w_out, w_gate, w_up, w_down, final_norm_g):
    S = x.shape[1]
    cos, sin = axial_rope_tables(S, x.dtype)
    silu_c = jax.nn.silu(c)
    silu_cc = jax.nn.silu(c_ctx)
    for l in range(DEPTH):
        last = l == DEPTH - 1
        mod = (silu_c @ w_mod[l] + b_mod[l])[:, None, :]
        mod_c = silu_cc @ w_mod[l] + b_mod[l]
        sh1, sc1, g1, sh2, sc2, g2 = jnp.split(mod, 6, axis=-1)
        sh1c, sc1c, g1c, sh2c, sc2c, g2c = jnp.split(mod_c, 6, axis=-1)
        h = modulate(rmsnorm(x, norm1_g[l]), sh1, sc1)
        hc = modulate(rmsnorm(ctx, norm1_g[l]), sh1c, sc1c)
        y, yc = hybrid_mixer(h, hc, w_in[l], conv_w[l], a_log[l], dt_bias[l], gdn_norm_g[l], q_norm_g[l],
                             w_uq[l], kv_norm_g[l], w_ukv[l], w_out[l], cos, sin, not last)
        x = x + g1 * y
        h2 = modulate(rmsnorm(x, norm2_g[l]), sh2, sc2)
        x = x + g2 * swiglu(h2, w_gate[l], w_up[l], w_down[l])
        if not last:
            ctx = ctx + g1c * yc
            hc2 = modulate(rmsnorm(ctx, norm2_g[l]), sh2c, sc2c)
            ctx = ctx + g2c * swiglu(hc2, w_gate[l], w_up[l], w_down[l])
    return rmsnorm(x, final_norm_g)
```

```python
import functools
import math

import jax
import jax.numpy as jnp
import numpy as np
from jax import lax
from jax.experimental import pallas as pl
from jax.experimental.pallas import tpu as pltpu

D_MODEL = 1024
DEPTH = 2
GRID_W = 64
EPS = 1e-6

GDN_HEADS = 8
GDN_DK = 64
GDN_DV = 64
GDN_KEY_W = GDN_HEADS * GDN_DK
GDN_VAL_W = GDN_HEADS * GDN_DV
CONV_W = 2 * GDN_KEY_W + GDN_VAL_W
CONV_K = 5
CHUNK = 64

MLA_HEADS = 8
QK_NOPE = 64
QK_ROPE = 32
V_DIM = 64
Q_LORA = 256
KV_LORA = 128
ROPE_THETA = 10000.0
AXIS_DIM = QK_ROPE // 2
FFN_HIDDEN = -(-8 * D_MODEL // (3 * 256)) * 256

LANES = 128
HEAD_SLOT = LANES
ATT_W = MLA_HEADS * HEAD_SLOT
IN_W = CONV_W + GDN_VAL_W + Q_LORA + KV_LORA + LANES
VMEM_LIMIT = 56 * 1024 * 1024

F32 = jnp.float32
BF16 = jnp.bfloat16
HI = lax.Precision.HIGHEST


def _silu(x):
    return x * (1.0 / (1.0 + jnp.exp(-x)))


def _rms(x, g):
    return x * lax.rsqrt(jnp.mean(x * x, axis=-1, keepdims=True) + EPS) * g


def _mod_kernel(c_ref, w_ref, b_ref, o_ref):
    s = _silu(c_ref[...]).astype(BF16)
    o_ref[...] = jnp.dot(s, w_ref[...], preferred_element_type=F32) + b_ref[...]


def _modulation(cc, w_mod, b_mod):
    rows, d = cc.shape
    n = w_mod.shape[1]
    tn = 1024
    return pl.pallas_call(
        _mod_kernel,
        grid=(n // tn,),
        in_specs=[pl.BlockSpec((rows, d), lambda j: (0, 0)),
                  pl.BlockSpec((d, tn), lambda j: (0, j)),
                  pl.BlockSpec((1, tn), lambda j: (0, j))],
        out_specs=pl.BlockSpec((rows, tn), lambda j: (0, j)),
        out_shape=jax.ShapeDtypeStruct((rows, n), F32),
        compiler_params=pltpu.CompilerParams(dimension_semantics=("arbitrary",),
                                             vmem_limit_bytes=VMEM_LIMIT),
        name="modulation",
    )(cc, w_mod, b_mod)


def _in_kernel(x_ref, sh_ref, sc_ref, g_ref, win_ref, qg_ref, wq_ref, kvg_ref, wkv_ref, vone_ref, tab_ref,
               qkv_ref, z_ref, misc_ref, q_ref, k_ref, v_ref):
    x = x_ref[...]
    h = _rms(x, g_ref[...]) * (1.0 + sc_ref[0]) + sh_ref[0]
    p = jnp.dot(h.astype(BF16), win_ref[...], preferred_element_type=F32)
    o1 = CONV_W
    o2 = o1 + GDN_VAL_W
    o3 = o2 + Q_LORA
    o4 = o3 + KV_LORA
    qkv_ref[...] = p[:, :o1]
    z_ref[...] = p[:, o1:o2]
    misc = p[:, o4:]
    misc_ref[...] = misc
    cqn = _rms(p[:, o2:o3], qg_ref[...]).astype(BF16)
    ckvn = _rms(p[:, o3:o4], kvg_ref[...]).astype(BF16)
    qa = jnp.dot(cqn, wq_ref[...], preferred_element_type=F32)
    kv = jnp.dot(ckvn, wkv_ref[...], preferred_element_type=F32)
    tab = tab_ref[...]
    cosq, sinq = tab[:, 0:LANES], tab[:, LANES:2 * LANES]
    cosk, sink = tab[:, 2 * LANES:3 * LANES], tab[:, 3 * LANES:4 * LANES]
    kpe = misc * cosk + pltpu.roll(misc, LANES - QK_ROPE, 1) * sink
    for hd in range(MLA_HEADS):
        sl = slice(hd * HEAD_SLOT, (hd + 1) * HEAD_SLOT)
        qh = qa[:, sl]
        q_ref[:, sl] = (qh * cosq + pltpu.roll(qh, LANES - QK_ROPE, 1) * sinq).astype(BF16)
        k_ref[:, sl] = (kv[:, sl] + kpe).astype(BF16)
    v_ref[...] = (kv[:, ATT_W:] + vone_ref[...]).astype(BF16)


def _in_proj(x, sh, sc, g, win, qg, wq, kvg, wkv, vone, tab, seq, tm):
    n, d = x.shape
    per_b = seq // tm
    per_batch_mod = sh.shape[0] > 1
    mod_map = (lambda i: (i // per_b, 0, 0)) if per_batch_mod else (lambda i: (0, 0, 0))
    const = lambda i: (0, 0)
    row = lambda i: (i, 0)
    out_shapes = (jax.ShapeDtypeStruct((n, CONV_W), F32), jax.ShapeDtypeStruct((n, GDN_VAL_W), F32),
                  jax.ShapeDtypeStruct((n, LANES), F32), jax.ShapeDtypeStruct((n, ATT_W), BF16),
                  jax.ShapeDtypeStruct((n, ATT_W), BF16), jax.ShapeDtypeStruct((n, ATT_W), BF16))
    return pl.pallas_call(
        _in_kernel,
        grid=(n // tm,),
        in_specs=[pl.BlockSpec((tm, d), row),
                  pl.BlockSpec((1, 1, d), mod_map), pl.BlockSpec((1, 1, d), mod_map),
                  pl.BlockSpec((1, d), const), pl.BlockSpec((d, IN_W), const),
                  pl.BlockSpec((1, Q_LORA), const), pl.BlockSpec((Q_LORA, ATT_W), const),
                  pl.BlockSpec((1, KV_LORA), const), pl.BlockSpec((KV_LORA, 2 * ATT_W), const),
                  pl.BlockSpec((1, ATT_W), const),
                  pl.BlockSpec((tm, 4 * LANES), lambda i: (i % per_b, 0))],
        out_specs=[pl.BlockSpec((tm, CONV_W), row), pl.BlockSpec((tm, GDN_VAL_W), row),
                   pl.BlockSpec((tm, LANES), row), pl.BlockSpec((tm, ATT_W), row),
                   pl.BlockSpec((tm, ATT_W), row), pl.BlockSpec((tm, ATT_W), row)],
        out_shape=out_shapes,
        compiler_params=pltpu.CompilerParams(dimension_semantics=("arbitrary",),
                                             vmem_limit_bytes=VMEM_LIMIT),
        name="in_proj",
    )(x, sh, sc, g, win, qg, wq, kvg, wkv, vone, tab)


def _conv_silu(x, w):
    t = x.shape[0]
    pad = jnp.zeros((8, x.shape[1]), F32)
    xp = jnp.concatenate([pad, x, pad], axis=0)
    acc = None
    for j in range(CONV_K):
        off = 8 - CONV_K // 2 + j
        term = xp[off:off + t, :] * w[j:j + 1, :]
        acc = term if acc is None else acc + term
    return _silu(acc)


def _l2(x):
    return x * lax.rsqrt(jnp.sum(x * x, axis=-1, keepdims=True) + EPS)


def _gdn_kernel(qc_ref, kc_ref, vc_ref, mc_ref, zc_ref, ql_ref, kl_ref, vl_ref, ml_ref, zl_ref,
                wq_ref, wk_ref, wv_ref, alog_ref, dtb_ref, gn_ref,
                yc_ref, yl_ref,
                q_s, k_s, v_s, g_s, b_s, acc_s):
    pair = pl.program_id(1)
    c = CHUNK
    tc = qc_ref.shape[0]
    tl = ql_ref.shape[0]
    lane16 = lax.broadcasted_iota(jnp.int32, (1, 2 * GDN_HEADS), 1)
    ri = lax.broadcasted_iota(jnp.int32, (c, c), 0)
    ci = lax.broadcasted_iota(jnp.int32, (c, c), 1)
    incl = (ri >= ci, ri <= ci)
    strict = (ri > ci, ri < ci)
    cum = (incl[0].astype(F32), incl[1].astype(F32))

    def stage(q_ref, k_ref, v_ref, m_ref, base, t):
        uq = _conv_silu(q_ref[...], wq_ref[...])
        uk = _conv_silu(k_ref[...], wk_ref[...])
        uv = _conv_silu(v_ref[...], wv_ref[...])
        for j in range(2):
            hs = slice(j * GDN_DK, (j + 1) * GDN_DK)
            q_s[j, base:base + t, :] = _l2(uq[:, hs]) * (GDN_DK ** -0.5)
            k_s[j, base:base + t, :] = _l2(uk[:, hs])
            v_s[j, base:base + t, :] = uv[:, hs]
        m = m_ref[...]
        a = m[:, 0:2 * GDN_HEADS] + dtb_ref[...]
        sp = jnp.maximum(a, 0.0) + jnp.log(1.0 + jnp.exp(-jnp.abs(a)))
        g_s[base:base + t, :] = -jnp.exp(alog_ref[...]) * sp
        b_s[base:base + t, :] = 1.0 / (1.0 + jnp.exp(-m[:, 2 * GDN_HEADS:4 * GDN_HEADS]))

    stage(qc_ref, kc_ref, vc_ref, mc_ref, 0, tc)
    stage(ql_ref, kl_ref, vl_ref, ml_ref, tc, tl)
    acc_s[...] = jnp.zeros_like(acc_s)

    def chunk_step(j, d, start, s_state):
        rows = pl.ds(start, c)
        q = q_s[j, rows, :]
        k = k_s[j, rows, :]
        v = v_s[j, rows, :]
        onehot = (lane16 == d * GDN_HEADS + 2 * pair + j).astype(F32)
        gcol = jnp.sum(g_s[rows, :] * onehot, axis=-1, keepdims=True)
        bcol = jnp.sum(b_s[rows, :] * onehot, axis=-1, keepdims=True)
        gcb = jnp.dot(cum[d], jnp.broadcast_to(gcol, (c, LANES)), precision=HI,
                      preferred_element_type=F32)
        gc = gcb[:, 0:1]
        rmat = jnp.transpose(gcb)[0:c, :]
        decay = jnp.where(incl[d], jnp.exp(jnp.where(incl[d], gc - rmat, 0.0)), 0.0)
        kb = k * bcol
        pm = lax.dot_general(jnp.concatenate([kb, q], axis=0), k, (((1,), (1,)), ((), ())),
                             precision=HI, preferred_element_type=F32)
        amat = jnp.where(strict[d], pm[0:c] * decay, 0.0)
        qk = pm[c:2 * c] * decay
        egc = jnp.exp(gc)
        x = jnp.concatenate([v * bcol, kb * egc], axis=1)
        bm = -amat
        levels = int(math.log2(c))
        for lvl in range(levels):
            if lvl < levels - 1:
                y = jnp.dot(bm, jnp.concatenate([x, bm], axis=1), precision=HI, preferred_element_type=F32)
                x = x + y[:, 0:2 * GDN_DV]
                bm = y[:, 2 * GDN_DV:]
            else:
                x = x + jnp.dot(bm, x, precision=HI, preferred_element_type=F32)
        u = x[:, 0:GDN_DV]
        w = x[:, GDN_DV:]
        gc_last = gc[c - 1:c, :] if d == 0 else gc[0:1, :]
        qg = q * egc
        kg = k * jnp.exp(gc_last - gc)
        ws = jnp.dot(jnp.concatenate([w, qg], axis=0), s_state, precision=HI, preferred_element_type=F32)
        v_new = u - ws[0:c]
        o = ws[c:2 * c] + jnp.dot(qk, v_new, precision=HI, preferred_element_type=F32)
        s_new = s_state * jnp.exp(gc_last) + lax.dot_general(
            kg, v_new, (((0,), (0,)), ((), ())), precision=HI, preferred_element_type=F32)
        hs = slice(j * GDN_DV, (j + 1) * GDN_DV)
        acc_s[rows, hs] = acc_s[rows, hs] + o
        return s_new

    def run(base, nchunks, states):
        def body(i, st):
            new = []
            for j in range(2):
                for d in range(2):
                    idx = i if d == 0 else nchunks - 1 - i
                    start = pl.multiple_of(base + idx * c, c)
                    new.append(chunk_step(j, d, start, st[2 * j + d]))
            return tuple(new)
        return lax.fori_loop(0, nchunks, body, states)

    zero = jnp.zeros((GDN_DK, GDN_DV), F32)
    states = run(0, tc // c, (zero, zero, zero, zero))
    run(tc, tl // c, states)

    def finish(z_ref, y_ref, base, t):
        o = acc_s[base:base + t, :]
        z = z_ref[...]
        for j in range(2):
            hs = slice(j * GDN_DV, (j + 1) * GDN_DV)
            y = _rms(o[:, hs], gn_ref[...]) * _silu(z[:, hs])
            y_ref[:, hs] = y.astype(y_ref.dtype)

    finish(zc_ref, yc_ref, 0, tc)
    finish(zl_ref, yl_ref, tc, tl)


def _gdn(qkv_c, misc_c, z_c, qkv_l, misc_l, z_l, conv_w, alog16, dtb16, gn, batch, tc, tl):
    npair = GDN_HEADS // 2
    kq, kk, kvv = 0, npair, 2 * npair
    blk = lambda t, off: pl.BlockSpec((t, LANES), lambda b, p: (b, off + p))
    mblk = lambda t: pl.BlockSpec((t, LANES), lambda b, p: (b, 0))
    wblk = lambda off: pl.BlockSpec((CONV_K, LANES), lambda b, p: (0, off + p))
    const = lambda b, p: (0, 0)
    tt = tc + tl
    return pl.pallas_call(
        _gdn_kernel,
        grid=(batch, npair),
        in_specs=[blk(tc, kq), blk(tc, kk), blk(tc, kvv), mblk(tc), blk(tc, 0),
                  blk(tl, kq), blk(tl, kk), blk(tl, kvv), mblk(tl), blk(tl, 0),
                  wblk(kq), wblk(kk), wblk(kvv),
                  pl.BlockSpec((1, 2 * GDN_HEADS), const), pl.BlockSpec((1, 2 * GDN_HEADS), const),
                  pl.BlockSpec((1, GDN_DV), const)],
        out_specs=[blk(tc, 0), blk(tl, 0)],
        out_shape=(jax.ShapeDtypeStruct((batch * tc, GDN_VAL_W), BF16),
                   jax.ShapeDtypeStruct((batch * tl, GDN_VAL_W), BF16)),
        scratch_shapes=[pltpu.VMEM((2, tt, GDN_DK), F32), pltpu.VMEM((2, tt, GDN_DK), F32),
                        pltpu.VMEM((2, tt, GDN_DV), F32),
                        pltpu.VMEM((tt, 2 * GDN_HEADS), F32), pltpu.VMEM((tt, 2 * GDN_HEADS), F32),
                        pltpu.VMEM((tt, LANES), F32)],
        compiler_params=pltpu.CompilerParams(dimension_semantics=("arbitrary", "arbitrary"),
                                             vmem_limit_bytes=VMEM_LIMIT),
        name="gdn",
    )(qkv_c, qkv_c, qkv_c, misc_c, z_c, qkv_l, qkv_l, qkv_l, misc_l, z_l,
      conv_w, conv_w, conv_w, alog16, dtb16, gn)


def _attn_kernel(*refs, n_src):
    q_ref = refs[0]
    k_refs = refs[1:1 + n_src]
    v_refs = refs[1 + n_src:1 + 2 * n_src]
    o_ref = refs[1 + 2 * n_src]
    for j in range(2):
        sl = slice(j * HEAD_SLOT, (j + 1) * HEAD_SLOT)
        qh = q_ref[:, sl]
        scores = [lax.dot_general(qh, kr[:, sl], (((1,), (1,)), ((), ())), preferred_element_type=F32)
                  for kr in k_refs]
        m = None
        for s in scores:
            ms = jnp.max(s, axis=-1, keepdims=True)
            m = ms if m is None else jnp.maximum(m, ms)
        acc = None
        for s, vr in zip(scores, v_refs):
            p = jnp.exp(s - m).astype(BF16)
            t = jnp.dot(p, vr[:, sl], preferred_element_type=F32)
            acc = t if acc is None else acc + t
        out = acc[:, 0:V_DIM] / acc[:, V_DIM:V_DIM + 1]
        o_ref[:, j * V_DIM:(j + 1) * V_DIM] = out.astype(o_ref.dtype)


def _attention(q, ks, vs, batch, tq_total, tq, src_lens):
    n_src = len(ks)
    nq = tq_total // tq
    npair = MLA_HEADS // 2
    qmap = lambda b, p, i: (b * nq + i, p)
    kvmap = lambda b, p, i: (b, p)
    in_specs = [pl.BlockSpec((tq, 2 * HEAD_SLOT), qmap)]
    in_specs += [pl.BlockSpec((t, 2 * HEAD_SLOT), kvmap) for t in src_lens]
    in_specs += [pl.BlockSpec((t, 2 * HEAD_SLOT), kvmap) for t in src_lens]
    return pl.pallas_call(
        functools.partial(_attn_kernel, n_src=n_src),
        grid=(batch, npair, nq),
        in_specs=in_specs,
        out_specs=pl.BlockSpec((tq, 2 * V_DIM), qmap),
        out_shape=jax.ShapeDtypeStruct((batch * tq_total, MLA_HEADS * V_DIM), BF16),
        compiler_params=pltpu.CompilerParams(dimension_semantics=("arbitrary", "arbitrary", "arbitrary"),
                                             vmem_limit_bytes=VMEM_LIMIT),
        name="attention",
    )(q, *ks, *vs)


def _out_ffn_kernel(x_ref, yg_ref, ym_ref, wo_ref, g1_ref, sh2_ref, sc2_ref, g2_ref, n2_ref,
                    wg_ref, wu_ref, wd_ref, fn_ref, o_ref, x1_s, h2_s, acc_s, *, final_norm):
    j = pl.program_id(1)

    @pl.when(j == 0)
    def _():
        half = GDN_VAL_W
        y = (jnp.dot(yg_ref[...], wo_ref[0:half, :], preferred_element_type=F32)
             + jnp.dot(ym_ref[...], wo_ref[half:, :], preferred_element_type=F32))
        x1 = x_ref[...] + g1_ref[0] * y
        x1_s[...] = x1
        h2 = _rms(x1, n2_ref[...]) * (1.0 + sc2_ref[0]) + sh2_ref[0]
        h2_s[...] = h2.astype(BF16)
        acc_s[...] = jnp.zeros_like(acc_s)

    h2 = h2_s[...]
    gate = jnp.dot(h2, wg_ref[...], preferred_element_type=F32)
    up = jnp.dot(h2, wu_ref[...], preferred_element_type=F32)
    act = (_silu(gate) * up).astype(BF16)
    acc_s[...] += jnp.dot(act, wd_ref[...], preferred_element_type=F32)

    @pl.when(j == pl.num_programs(1) - 1)
    def _():
        x2 = x1_s[...] + g2_ref[0] * acc_s[...]
        if final_norm:
            x2 = _rms(x2, fn_ref[...])
        o_ref[...] = x2


def _out_ffn(x, yg, ym, wo, g1, sh2, sc2, g2, n2, wg, wu, wd, fn, seq, tm, th, final_norm):
    n, d = x.shape
    per_b = seq // tm
    per_batch_mod = g1.shape[0] > 1
    mod_map = (lambda i, j: (i // per_b, 0, 0)) if per_batch_mod else (lambda i, j: (0, 0, 0))
    const = lambda i, j: (0, 0)
    row = lambda i, j: (i, 0)
    hid = FFN_HIDDEN
    mod_spec = pl.BlockSpec((1, 1, d), mod_map)
    return pl.pallas_call(
        functools.partial(_out_ffn_kernel, final_norm=final_norm),
        grid=(n // tm, hid // th),
        in_specs=[pl.BlockSpec((tm, d), row),
                  pl.BlockSpec((tm, GDN_VAL_W), row), pl.BlockSpec((tm, MLA_HEADS * V_DIM), row),
                  pl.BlockSpec((d, d), const),
                  mod_spec, mod_spec, mod_spec, mod_spec,
                  pl.BlockSpec((1, d), const),
                  pl.BlockSpec((d, th), lambda i, j: (0, j)), pl.BlockSpec((d, th), lambda i, j: (0, j)),
                  pl.BlockSpec((th, d), lambda i, j: (j, 0)),
                  pl.BlockSpec((1, d), const)],
        out_specs=pl.BlockSpec((tm, d), row),
        out_shape=jax.ShapeDtypeStruct((n, d), F32),
        scratch_shapes=[pltpu.VMEM((tm, d), F32), pltpu.VMEM((tm, d), BF16), pltpu.VMEM((tm, d), F32)],
        compiler_params=pltpu.CompilerParams(dimension_semantics=("arbitrary", "arbitrary"),
                                             vmem_limit_bytes=VMEM_LIMIT),
        name="out_ffn",
    )(x, yg, ym, wo, g1, sh2, sc2, g2, n2, wg, wu, wd, fn)


_ROT_PERM = np.concatenate([np.arange(8, 16), np.arange(0, 8), np.arange(24, 32), np.arange(16, 24)])
_ROT_SIGN = np.concatenate([-np.ones(8), np.ones(8), -np.ones(8), np.ones(8)]).astype(np.float32)


def _rot_cols(w):
    return w[..., _ROT_PERM] * _ROT_SIGN


def _arrange_w_in(w_in):
    o = np.cumsum([0, CONV_W, GDN_VAL_W, 2 * GDN_HEADS, 2 * GDN_HEADS, Q_LORA, KV_LORA, QK_ROPE])
    qkv, z, a, b, cq, ckv, kr = (w_in[:, o[i]:o[i + 1]] for i in range(7))
    misc = jnp.concatenate([a, b, jnp.zeros((w_in.shape[0], LANES - 4 * GDN_HEADS - 2 * QK_ROPE), w_in.dtype),
                            kr, _rot_cols(kr)], axis=1)
    return jnp.concatenate([qkv, z, cq, ckv, misc], axis=1).astype(BF16)


def _arrange_w_uq(w_uq):
    w = w_uq.reshape(Q_LORA, MLA_HEADS, QK_NOPE + QK_ROPE)
    pe = w[:, :, QK_NOPE:]
    return jnp.concatenate([w[:, :, :QK_NOPE], pe, _rot_cols(pe)], axis=-1).reshape(Q_LORA, ATT_W).astype(BF16)


def _arrange_w_ukv(w_ukv):
    w = w_ukv.reshape(KV_LORA, MLA_HEADS, QK_NOPE + V_DIM)
    zk = jnp.zeros((KV_LORA, MLA_HEADS, HEAD_SLOT - QK_NOPE), w.dtype)
    zv = jnp.zeros((KV_LORA, MLA_HEADS, HEAD_SLOT - V_DIM), w.dtype)
    wk = jnp.concatenate([w[:, :, :QK_NOPE], zk], axis=-1).reshape(KV_LORA, ATT_W)
    wv = jnp.concatenate([w[:, :, QK_NOPE:], zv], axis=-1).reshape(KV_LORA, ATT_W)
    return jnp.concatenate([wk, wv], axis=1).astype(BF16)


def _rope_tables(t_lat, t_ctx):
    rows = t_lat // GRID_W
    row = jnp.repeat(jnp.arange(rows), GRID_W).astype(F32)
    col = jnp.tile(jnp.arange(GRID_W), rows).astype(F32)
    inv_freq = ROPE_THETA ** (-jnp.arange(0, AXIS_DIM, 2, dtype=F32) / AXIS_DIM)

    def axis_angles(pos):
        ang = pos[:, None] * inv_freq[None, :]
        return jnp.concatenate([ang, ang], axis=-1)

    ang = jnp.concatenate([axis_angles(row), axis_angles(col)], axis=-1)
    cos, sin = jnp.cos(ang), jnp.sin(ang)
    scale = (QK_NOPE + QK_ROPE) ** -0.5

    def build(cos_t, sin_t):
        t = cos_t.shape[0]
        z32 = jnp.zeros((t, QK_ROPE), F32)
        z64 = jnp.zeros((t, QK_NOPE), F32)
        one64 = jnp.ones((t, QK_NOPE), F32)
        cosq = jnp.concatenate([one64, cos_t, z32], axis=1) * scale
        sinq = jnp.concatenate([z64, sin_t, z32], axis=1) * scale
        cosk = jnp.concatenate([z64, cos_t, z32], axis=1)
        sink = jnp.concatenate([z64, sin_t, z32], axis=1)
        return jnp.concatenate([cosq, sinq, cosk, sink], axis=1)

    lat = build(cos, sin)
    ctx = build(jnp.ones((t_ctx, QK_ROPE), F32), jnp.zeros((t_ctx, QK_ROPE), F32))
    return lat, ctx


def kernel(x, c, ctx, c_ctx, w_mod, b_mod, norm1_g, norm2_g, w_in, conv_w, a_log, dt_bias, gdn_norm_g,
           q_norm_g, w_uq, kv_norm_g, w_ukv, w_out, w_gate, w_up, w_down, final_norm_g):
    batch, seq, d = x.shape
    tctx = ctx.shape[1]
    tab_l, tab_c = _rope_tables(seq, tctx)
    vone = jnp.zeros((MLA_HEADS, HEAD_SLOT), F32).at[:, V_DIM].set(1.0).reshape(1, ATT_W)
    cc = jnp.concatenate([c, c_ctx[None, :], jnp.zeros((16 - batch - 1, d), F32)], axis=0)
    xs = x.reshape(batch * seq, d)
    cs = ctx.reshape(batch * tctx, d)
    tm_l, tm_c, th = 256, 256, 1408

    for l in range(DEPTH):
        last = l == DEPTH - 1
        mod = _modulation(cc, w_mod[l].astype(BF16), b_mod[l][None, :])
        parts = [mod[:, i * d:(i + 1) * d] for i in range(6)]
        lat = [p[:batch].reshape(batch, 1, d) for p in parts]
        cxt = [p[batch:batch + 1].reshape(1, 1, d) for p in parts]
        win = _arrange_w_in(w_in[l])
        wq = _arrange_w_uq(w_uq[l])
        wkv = _arrange_w_ukv(w_ukv[l])
        n1 = norm1_g[l][None, :]
        qg = q_norm_g[l][None, :]
        kvg = kv_norm_g[l][None, :]

        qkv_c, z_c, misc_c, q_c, k_c, v_c = _in_proj(cs, cxt[0], cxt[1], n1, win, qg, wq, kvg, wkv, vone,
                                                     tab_c, tctx, tm_c)
        qkv_l, z_l, misc_l, q_l, k_l, v_l = _in_proj(xs, lat[0], lat[1], n1, win, qg, wq, kvg, wkv, vone,
                                                     tab_l, seq, tm_l)
        yg_c, yg_l = _gdn(qkv_c, misc_c, z_c, qkv_l, misc_l, z_l, conv_w[l],
                          a_log[l].reshape(1, -1), dt_bias[l].reshape(1, -1), gdn_norm_g[l][None, :],
                          batch, tctx, seq)
        ym_l = _attention(q_l, (k_c, k_l), (v_c, v_l), batch, seq, 512, (tctx, seq))

        wo = w_out[l].astype(BF16)
        wg, wu, wd = w_gate[l].astype(BF16), w_up[l].astype(BF16), w_down[l].astype(BF16)
        n2 = norm2_g[l][None, :]
        fn = final_norm_g[None, :]
        xs = _out_ffn(xs, yg_l, ym_l, wo, lat[2], lat[3], lat[4], lat[5], n2, wg, wu, wd, fn,
                      seq, tm_l * 2, th, last)
        if not last:
            ym_c = _attention(q_c, (k_c,), (v_c,), batch, tctx, tctx, (tctx,))
            cs = _out_ffn(cs, yg_c, ym_c, wo, cxt[2], cxt[3], cxt[4], cxt[5], n2, wg, wu, wd, fn,
                          tctx, tm_c, th, False)
    return xs.reshape(batch, seq, d)
```

```python
import functools
import math

import jax
import jax.numpy as jnp
import numpy as np
from jax import lax
from jax.experimental import pallas as pl
from jax.experimental.pallas import tpu as pltpu

D_MODEL = 1024
DEPTH = 2
GRID_W = 64
EPS = 1e-6

GDN_HEADS = 8
GDN_DK = 64
GDN_DV = 64
GDN_KEY_W = GDN_HEADS * GDN_DK
GDN_VAL_W = GDN_HEADS * GDN_DV
CONV_W = 2 * GDN_KEY_W + GDN_VAL_W
CONV_K = 5
CHUNK = 64
GDN_SUB = 4

MLA_HEADS = 8
QK_NOPE = 64
QK_ROPE = 32
V_DIM = 64
Q_LORA = 256
KV_LORA = 128
ROPE_THETA = 10000.0
AXIS_DIM = QK_ROPE // 2
FFN_HIDDEN = -(-8 * D_MODEL // (3 * 256)) * 256

LANES = 128
HEAD_SLOT = LANES
ATT_W = MLA_HEADS * HEAD_SLOT
IN_W = CONV_W + GDN_VAL_W + Q_LORA + KV_LORA + LANES
VMEM_LIMIT = 56 * 1024 * 1024

F32 = jnp.float32
BF16 = jnp.bfloat16
HI = lax.Precision.HIGHEST


def _silu(x):
    return x * (1.0 / (1.0 + jnp.exp(-x)))


def _rms(x, g):
    return x * lax.rsqrt(jnp.mean(x * x, axis=-1, keepdims=True) + EPS) * g


def _mod_kernel(c_ref, w_ref, b_ref, o_ref):
    s = _silu(c_ref[...]).astype(BF16)
    o_ref[...] = jnp.dot(s, w_ref[...], preferred_element_type=F32) + b_ref[...]


def _modulation(cc, w_mod, b_mod):
    rows, d = cc.shape
    n = w_mod.shape[1]
    tn = 1024
    return pl.pallas_call(
        _mod_kernel,
        grid=(n // tn,),
        in_specs=[pl.BlockSpec((rows, d), lambda j: (0, 0)),
                  pl.BlockSpec((d, tn), lambda j: (0, j)),
                  pl.BlockSpec((1, tn), lambda j: (0, j))],
        out_specs=pl.BlockSpec((rows, tn), lambda j: (0, j)),
        out_shape=jax.ShapeDtypeStruct((rows, n), F32),
        compiler_params=pltpu.CompilerParams(dimension_semantics=("arbitrary",),
                                             vmem_limit_bytes=VMEM_LIMIT),
        name="modulation",
    )(cc, w_mod, b_mod)


def _in_kernel(x_ref, sh_ref, sc_ref, g_ref, win_ref, qg_ref, wq_ref, kvg_ref, wkv_ref, vone_ref, tab_ref,
               qkv_ref, z_ref, misc_ref, q_ref, k_ref, v_ref):
    x = x_ref[...]
    h = _rms(x, g_ref[...]) * (1.0 + sc_ref[0]) + sh_ref[0]
    p = jnp.dot(h.astype(BF16), win_ref[...], preferred_element_type=F32)
    o1 = CONV_W
    o2 = o1 + GDN_VAL_W
    o3 = o2 + Q_LORA
    o4 = o3 + KV_LORA
    qkv_ref[...] = p[:, :o1]
    z_ref[...] = p[:, o1:o2]
    misc = p[:, o4:]
    misc_ref[...] = misc
    cqn = _rms(p[:, o2:o3], qg_ref[...]).astype(BF16)
    ckvn = _rms(p[:, o3:o4], kvg_ref[...]).astype(BF16)
    qa = jnp.dot(cqn, wq_ref[...], preferred_element_type=F32)
    kv = jnp.dot(ckvn, wkv_ref[...], preferred_element_type=F32)
    tab = tab_ref[...]
    cosq, sinq = tab[:, 0:LANES], tab[:, LANES:2 * LANES]
    cosk, sink = tab[:, 2 * LANES:3 * LANES], tab[:, 3 * LANES:4 * LANES]
    kpe = misc * cosk + pltpu.roll(misc, LANES - QK_ROPE, 1) * sink
    for hd in range(MLA_HEADS):
        sl = slice(hd * HEAD_SLOT, (hd + 1) * HEAD_SLOT)
        qh = qa[:, sl]
        q_ref[:, sl] = (qh * cosq + pltpu.roll(qh, LANES - QK_ROPE, 1) * sinq).astype(BF16)
        k_ref[:, sl] = (kv[:, sl] + kpe).astype(BF16)
    v_ref[...] = (kv[:, ATT_W:] + vone_ref[...]).astype(BF16)


def _in_proj(x, sh, sc, g, win, qg, wq, kvg, wkv, vone, tab, seq, tm):
    n, d = x.shape
    per_b = seq // tm
    per_batch_mod = sh.shape[0] > 1
    mod_map = (lambda i: (i // per_b, 0, 0)) if per_batch_mod else (lambda i: (0, 0, 0))
    const = lambda i: (0, 0)
    row = lambda i: (i, 0)
    out_shapes = (jax.ShapeDtypeStruct((n, CONV_W), F32), jax.ShapeDtypeStruct((n, GDN_VAL_W), F32),
                  jax.ShapeDtypeStruct((n, LANES), F32), jax.ShapeDtypeStruct((n, ATT_W), BF16),
                  jax.ShapeDtypeStruct((n, ATT_W), BF16), jax.ShapeDtypeStruct((n, ATT_W), BF16))
    return pl.pallas_call(
        _in_kernel,
        grid=(n // tm,),
        in_specs=[pl.BlockSpec((tm, d), row),
                  pl.BlockSpec((1, 1, d), mod_map), pl.BlockSpec((1, 1, d), mod_map),
                  pl.BlockSpec((1, d), const), pl.BlockSpec((d, IN_W), const),
                  pl.BlockSpec((1, Q_LORA), const), pl.BlockSpec((Q_LORA, ATT_W), const),
                  pl.BlockSpec((1, KV_LORA), const), pl.BlockSpec((KV_LORA, 2 * ATT_W), const),
                  pl.BlockSpec((1, ATT_W), const),
                  pl.BlockSpec((tm, 4 * LANES), lambda i: (i % per_b, 0))],
        out_specs=[pl.BlockSpec((tm, CONV_W), row), pl.BlockSpec((tm, GDN_VAL_W), row),
                   pl.BlockSpec((tm, LANES), row), pl.BlockSpec((tm, ATT_W), row),
                   pl.BlockSpec((tm, ATT_W), row), pl.BlockSpec((tm, ATT_W), row)],
        out_shape=out_shapes,
        compiler_params=pltpu.CompilerParams(dimension_semantics=("arbitrary",),
                                             vmem_limit_bytes=VMEM_LIMIT),
        name="in_proj",
    )(x, sh, sc, g, win, qg, wq, kvg, wkv, vone, tab)


def _conv_silu(x, w):
    t = x.shape[0]
    pad = jnp.zeros((8, x.shape[1]), F32)
    xp = jnp.concatenate([pad, x, pad], axis=0)
    acc = None
    for j in range(CONV_K):
        off = 8 - CONV_K // 2 + j
        term = xp[off:off + t, :] * w[j:j + 1, :]
        acc = term if acc is None else acc + term
    return _silu(acc)


def _l2(x):
    return x * lax.rsqrt(jnp.sum(x * x, axis=-1, keepdims=True) + EPS)


def _hi_lo(x):
    hi = x.astype(BF16).astype(F32)
    return hi, (x - hi).astype(BF16).astype(F32)


def _lhs3(x):
    hi, lo = _hi_lo(x)
    return jnp.concatenate([hi, lo, hi], axis=1).astype(BF16)


def _rhs3(x):
    hi, lo = _hi_lo(x)
    return jnp.concatenate([hi, hi, lo], axis=0).astype(BF16)


def _mm(a, b):
    return jnp.dot(a, b, preferred_element_type=F32)


def _tri_inverse(amats, lvl_mask, eye):
    n1 = [jnp.where(lvl_mask[0], -a, 0.0) for a in amats]
    n2 = [_mm(_lhs3(n), _rhs3(n)) for n in n1]
    n2r = [_rhs3(n) for n in n2]
    n4 = [_mm(_lhs3(n), r) for n, r in zip(n2, n2r)]
    t = [eye + n for n in n1]
    t = [x + _mm(_lhs3(x), r) for x, r in zip(t, n2r)]
    t = [x + _mm(_lhs3(x), _rhs3(n)) for x, n in zip(t, n4)]
    for m in lvl_mask[1:]:
        tq = [_mm(_lhs3(x), _rhs3(jnp.where(m, a, 0.0))) for x, a in zip(t, amats)]
        t = [x - _mm(_lhs3(y), _rhs3(x)) for x, y in zip(t, tq)]
    return t


def _gdn_kernel(qc_ref, kc_ref, vc_ref, mc_ref, zc_ref, ql_ref, kl_ref, vl_ref, ml_ref, zl_ref,
                wq_ref, wk_ref, wv_ref, alog_ref, dtb_ref, gn_ref,
                yc_ref, yl_ref,
                q_s, k_s, v_s, g_s, b_s, acc_s, gc_s, gct_s):
    pair = pl.program_id(1)
    c = CHUNK
    tc = qc_ref.shape[0]
    tl = ql_ref.shape[0]
    ngate = 2 * GDN_HEADS
    lane16 = lax.broadcasted_iota(jnp.int32, (1, ngate), 1)
    lane128 = lax.broadcasted_iota(jnp.int32, (1, LANES), 1)
    ri = lax.broadcasted_iota(jnp.int32, (c, c), 0)
    ci = lax.broadcasted_iota(jnp.int32, (c, c), 1)
    incl = (ri >= ci, ri <= ci)
    strict = (ri > ci, ri < ci)
    eye = (ri == ci).astype(F32)
    bi, bj = ri >> 3, ci >> 3
    lvl_mask = (bi == bj,
                ((bi >> 1) == (bj >> 1)) & (bi != bj),
                ((bi >> 2) == (bj >> 2)) & ((bi >> 1) != (bj >> 1)),
                (bi >> 2) != (bj >> 2))
    r2 = lax.broadcasted_iota(jnp.int32, (2 * c, c), 0)
    c2 = lax.broadcasted_iota(jnp.int32, (2 * c, c), 1)
    cum2 = jnp.where(r2 < c, (c2 <= r2).astype(F32), (c2 >= r2 - c).astype(F32))

    def stage(q_ref, k_ref, v_ref, m_ref, base, t):
        uq = _conv_silu(q_ref[...], wq_ref[...])
        uk = _conv_silu(k_ref[...], wk_ref[...])
        uv = _conv_silu(v_ref[...], wv_ref[...])
        for j in range(2):
            hs = slice(j * GDN_DK, (j + 1) * GDN_DK)
            q_s[j, base:base + t, :] = _l2(uq[:, hs]) * (GDN_DK ** -0.5)
            k_s[j, base:base + t, :] = _l2(uk[:, hs])
            v_s[j, base:base + t, :] = uv[:, hs]
        m = m_ref[...]
        a = m[:, 0:2 * GDN_HEADS] + dtb_ref[...]
        sp = jnp.maximum(a, 0.0) + jnp.log(1.0 + jnp.exp(-jnp.abs(a)))
        g_s[base:base + t, :] = -jnp.exp(alog_ref[...]) * sp
        b_s[base:base + t, :] = 1.0 / (1.0 + jnp.exp(-m[:, 2 * GDN_HEADS:4 * GDN_HEADS]))

    stage(qc_ref, kc_ref, vc_ref, mc_ref, 0, tc)
    stage(ql_ref, kl_ref, vl_ref, ml_ref, tc, tl)
    acc_s[...] = jnp.zeros_like(acc_s)
    gc_s[...] = jnp.zeros_like(gc_s)

    def cumsums(i, carry):
        rows = pl.ds(pl.multiple_of(i * c, c), c)
        cs = jnp.dot(cum2, g_s[rows, :], precision=HI, preferred_element_type=F32)
        gc_s[rows, 0:ngate] = cs[0:c]
        gc_s[rows, ngate:2 * ngate] = cs[c:2 * c]
        trow = pl.ds(pl.multiple_of(i * 2 * ngate, 2 * ngate), 2 * ngate)
        gct_s[trow, :] = jnp.transpose(gc_s[rows, :])[0:2 * ngate, :]
        return carry

    lax.fori_loop(0, (tc + tl) // c, cumsums, 0)

    chains = [(j, d) for u in range(GDN_SUB) for j in range(2) for d in range(2)]
    subidx = [u for u in range(GDN_SUB) for j in range(2) for d in range(2)]

    def chunk_step(starts, cidxs, states):
        n = range(len(chains))
        rows = [pl.ds(s, c) for s in starts]
        q = [q_s[j, r, :] for (j, d), r in zip(chains, rows)]
        k = [k_s[j, r, :] for (j, d), r in zip(chains, rows)]
        v = [v_s[j, r, :] for (j, d), r in zip(chains, rows)]
        head = [2 * pair + j for j, d in chains]
        glane = [d * (ngate + GDN_HEADS) + h for (j, d), h in zip(chains, head)]
        gc = [jnp.sum(jnp.where(lane128 == gl, gc_s[r, :], 0.0), axis=-1, keepdims=True)
              for gl, r in zip(glane, rows)]
        bcol = [jnp.sum(jnp.where(lane16 == d * GDN_HEADS + h, b_s[r, :], 0.0), axis=-1, keepdims=True)
                for (j, d), h, r in zip(chains, head, rows)]
        grow = [gct_s[pl.ds(ci_ * 2 * ngate + gl, 1), :] for ci_, gl in zip(cidxs, glane)]
        decay = [jnp.where(incl[d], jnp.exp(jnp.where(incl[d], gc[i] - grow[i], 0.0)), 0.0)
                 for i, (j, d) in enumerate(chains)]
        kb = [k[i] * bcol[i] for i in n]
        pm = [lax.dot_general(jnp.concatenate([kb[i], q[i]], axis=0).astype(BF16), k[i].astype(BF16),
                              (((1,), (1,)), ((), ())), preferred_element_type=F32) for i in n]
        amat = [jnp.where(strict[d], pm[i][0:c] * decay[i], 0.0) for i, (j, d) in enumerate(chains)]
        qk = [(pm[i][c:2 * c] * decay[i]).astype(BF16) for i in n]
        tinv = _tri_inverse(amat, lvl_mask, eye)
        egc = [jnp.exp(g) for g in gc]
        gc_last = [gc[i][c - 1:c, :] if d == 0 else gc[i][0:1, :] for i, (j, d) in enumerate(chains)]
        rhs = [jnp.concatenate([kb[i] * egc[i], v[i] * bcol[i]], axis=1).astype(BF16) for i in n]
        wu = [_mm(tinv[i].astype(BF16), rhs[i]).astype(BF16) for i in n]
        kgt = [jnp.transpose(k[i] * jnp.exp(gc_last[i] - gc[i])).astype(BF16) for i in n]
        qkwu = [_mm(qk[i], wu[i]) for i in n]
        kgwu = [_mm(kgt[i], wu[i]) for i in n]
        lhs = [jnp.concatenate([-kgwu[i][:, 0:GDN_DK], q[i] * egc[i] - qkwu[i][:, 0:GDN_DK]], axis=0).astype(BF16)
               for i in n]
        st = list(states)
        for u in range(GDN_SUB):
            sel = [i for i in n if subidx[i] == u]
            mqs = {i: _mm(lhs[i], st[2 * chains[i][0] + chains[i][1]].astype(BF16)) for i in sel}
            for i in sel:
                j, d = chains[i]
                st[2 * j + d] = st[2 * j + d] * jnp.exp(gc_last[i]) + mqs[i][0:GDN_DK] + kgwu[i][:, GDN_DK:]
                hs = slice(j * GDN_DV, (j + 1) * GDN_DV)
                acc_s[rows[i], hs] = acc_s[rows[i], hs] + qkwu[i][:, GDN_DK:] + mqs[i][GDN_DK:]
        return tuple(st)

    def run(base, nchunks, states):
        def body(i, st):
            pos = [i * GDN_SUB + u for u in subidx]
            idx = [p if d == 0 else nchunks - 1 - p for p, (j, d) in zip(pos, chains)]
            starts = [pl.multiple_of(base + x * c, c) for x in idx]
            return chunk_step(starts, [base // c + x for x in idx], st)
        return lax.fori_loop(0, nchunks // GDN_SUB, body, states)

    zero = jnp.zeros((GDN_DK, GDN_DV), F32)
    states = run(0, tc // c, (zero, zero, zero, zero))
    run(tc, tl // c, states)

    def finish(z_ref, y_ref, base, t):
        o = acc_s[base:base + t, :]
        z = z_ref[...]
        for j in range(2):
            hs = slice(j * GDN_DV, (j + 1) * GDN_DV)
            y = _rms(o[:, hs], gn_ref[...]) * _silu(z[:, hs])
            y_ref[:, hs] = y.astype(y_ref.dtype)

    finish(zc_ref, yc_ref, 0, tc)
    finish(zl_ref, yl_ref, tc, tl)


def _gdn(qkv_c, misc_c, z_c, qkv_l, misc_l, z_l, conv_w, alog16, dtb16, gn, batch, tc, tl):
    npair = GDN_HEADS // 2
    kq, kk, kvv = 0, npair, 2 * npair
    blk = lambda t, off: pl.BlockSpec((t, LANES), lambda b, p: (b, off + p))
    mblk = lambda t: pl.BlockSpec((t, LANES), lambda b, p: (b, 0))
    wblk = lambda off: pl.BlockSpec((CONV_K, LANES), lambda b, p: (0, off + p))
    const = lambda b, p: (0, 0)
    tt = tc + tl
    return pl.pallas_call(
        _gdn_kernel,
        grid=(batch, npair),
        in_specs=[blk(tc, kq), blk(tc, kk), blk(tc, kvv), mblk(tc), blk(tc, 0),
                  blk(tl, kq), blk(tl, kk), blk(tl, kvv), mblk(tl), blk(tl, 0),
                  wblk(kq), wblk(kk), wblk(kvv),
                  pl.BlockSpec((1, 2 * GDN_HEADS), const), pl.BlockSpec((1, 2 * GDN_HEADS), const),
                  pl.BlockSpec((1, GDN_DV), const)],
        out_specs=[blk(tc, 0), blk(tl, 0)],
        out_shape=(jax.ShapeDtypeStruct((batch * tc, GDN_VAL_W), BF16),
                   jax.ShapeDtypeStruct((batch * tl, GDN_VAL_W), BF16)),
        scratch_shapes=[pltpu.VMEM((2, tt, GDN_DK), F32), pltpu.VMEM((2, tt, GDN_DK), F32),
                        pltpu.VMEM((2, tt, GDN_DV), F32),
                        pltpu.VMEM((tt, 2 * GDN_HEADS), F32), pltpu.VMEM((tt, 2 * GDN_HEADS), F32),
                        pltpu.VMEM((tt, LANES), F32), pltpu.VMEM((tt, LANES), F32),
                        pltpu.VMEM((tt // CHUNK * 4 * GDN_HEADS, CHUNK), F32)],
        compiler_params=pltpu.CompilerParams(dimension_semantics=("arbitrary", "arbitrary"),
                                             vmem_limit_bytes=VMEM_LIMIT),
        name="gdn",
    )(qkv_c, qkv_c, qkv_c, misc_c, z_c, qkv_l, qkv_l, qkv_l, misc_l, z_l,
      conv_w, conv_w, conv_w, alog16, dtb16, gn)


def _attn_kernel(*refs, n_src):
    q_ref = refs[0]
    k_refs = refs[1:1 + n_src]
    v_refs = refs[1 + n_src:1 + 2 * n_src]
    o_ref = refs[1 + 2 * n_src]
    for j in range(2):
        sl = slice(j * HEAD_SLOT, (j + 1) * HEAD_SLOT)
        qh = q_ref[:, sl]
        scores = [lax.dot_general(qh, kr[:, sl], (((1,), (1,)), ((), ())), preferred_element_type=F32)
                  for kr in k_refs]
        m = None
        for s in scores:
            ms = jnp.max(s, axis=-1, keepdims=True)
            m = ms if m is None else jnp.maximum(m, ms)
        acc = None
        for s, vr in zip(scores, v_refs):
            p = jnp.exp(s - m).astype(BF16)
            t = jnp.dot(p, vr[:, sl], preferred_element_type=F32)
            acc = t if acc is None else acc + t
        out = acc[:, 0:V_DIM] / acc[:, V_DIM:V_DIM + 1]
        o_ref[:, j * V_DIM:(j + 1) * V_DIM] = out.astype(o_ref.dtype)


def _attention(q, ks, vs, batch, tq_total, tq, src_lens):
    n_src = len(ks)
    nq = tq_total // tq
    npair = MLA_HEADS // 2
    qmap = lambda b, p, i: (b * nq + i, p)
    kvmap = lambda b, p, i: (b, p)
    in_specs = [pl.BlockSpec((tq, 2 * HEAD_SLOT), qmap)]
    in_specs += [pl.BlockSpec((t, 2 * HEAD_SLOT), kvmap) for t in src_lens]
    in_specs += [pl.BlockSpec((t, 2 * HEAD_SLOT), kvmap) for t in src_lens]
    return pl.pallas_call(
        functools.partial(_attn_kernel, n_src=n_src),
        grid=(batch, npair, nq),
        in_specs=in_specs,
        out_specs=pl.BlockSpec((tq, 2 * V_DIM), qmap),
        out_shape=jax.ShapeDtypeStruct((batch * tq_total, MLA_HEADS * V_DIM), BF16),
        compiler_params=pltpu.CompilerParams(dimension_semantics=("arbitrary", "arbitrary", "arbitrary"),
                                             vmem_limit_bytes=VMEM_LIMIT),
        name="attention",
    )(q, *ks, *vs)


def _out_ffn_kernel(x_ref, yg_ref, ym_ref, wo_ref, g1_ref, sh2_ref, sc2_ref, g2_ref, n2_ref,
                    wg_ref, wu_ref, wd_ref, fn_ref, o_ref, x1_s, h2_s, acc_s, *, final_norm):
    j = pl.program_id(1)

    @pl.when(j == 0)
    def _():
        half = GDN_VAL_W
        y = (jnp.dot(yg_ref[...], wo_ref[0:half, :], preferred_element_type=F32)
             + jnp.dot(ym_ref[...], wo_ref[half:, :], preferred_element_type=F32))
        x1 = x_ref[...] + g1_ref[0] * y
        x1_s[...] = x1
        h2 = _rms(x1, n2_ref[...]) * (1.0 + sc2_ref[0]) + sh2_ref[0]
        h2_s[...] = h2.astype(BF16)
        acc_s[...] = jnp.zeros_like(acc_s)

    h2 = h2_s[...]
    gate = jnp.dot(h2, wg_ref[...], preferred_element_type=F32)
    up = jnp.dot(h2, wu_ref[...], preferred_element_type=F32)
    act = (_silu(gate) * up).astype(BF16)
    acc_s[...] += jnp.dot(act, wd_ref[...], preferred_element_type=F32)

    @pl.when(j == pl.num_programs(1) - 1)
    def _():
        x2 = x1_s[...] + g2_ref[0] * acc_s[...]
        if final_norm:
            x2 = _rms(x2, fn_ref[...])
        o_ref[...] = x2


def _out_ffn(x, yg, ym, wo, g1, sh2, sc2, g2, n2, wg, wu, wd, fn, seq, tm, th, final_norm):
    n, d = x.shape
    per_b = seq // tm
    per_batch_mod = g1.shape[0] > 1
    mod_map = (lambda i, j: (i // per_b, 0, 0)) if per_batch_mod else (lambda i, j: (0, 0, 0))
    const = lambda i, j: (0, 0)
    row = lambda i, j: (i, 0)
    hid = FFN_HIDDEN
    mod_spec = pl.BlockSpec((1, 1, d), mod_map)
    return pl.pallas_call(
        functools.partial(_out_ffn_kernel, final_norm=final_norm),
        grid=(n // tm, hid // th),
        in_specs=[pl.BlockSpec((tm, d), row),
                  pl.BlockSpec((tm, GDN_VAL_W), row), pl.BlockSpec((tm, MLA_HEADS * V_DIM), row),
                  pl.BlockSpec((d, d), const),
                  mod_spec, mod_spec, mod_spec, mod_spec,
                  pl.BlockSpec((1, d), const),
                  pl.BlockSpec((d, th), lambda i, j: (0, j)), pl.BlockSpec((d, th), lambda i, j: (0, j)),
                  pl.BlockSpec((th, d), lambda i, j: (j, 0)),
                  pl.BlockSpec((1, d), const)],
        out_specs=pl.BlockSpec((tm, d), row),
        out_shape=jax.ShapeDtypeStruct((n, d), F32),
        scratch_shapes=[pltpu.VMEM((tm, d), F32), pltpu.VMEM((tm, d), BF16), pltpu.VMEM((tm, d), F32)],
        compiler_params=pltpu.CompilerParams(dimension_semantics=("arbitrary", "arbitrary"),
                                             vmem_limit_bytes=VMEM_LIMIT),
        name="out_ffn",
    )(x, yg, ym, wo, g1, sh2, sc2, g2, n2, wg, wu, wd, fn)


_ROT_PERM = np.concatenate([np.arange(8, 16), np.arange(0, 8), np.arange(24, 32), np.arange(16, 24)])
_ROT_SIGN = np.concatenate([-np.ones(8), np.ones(8), -np.ones(8), np.ones(8)]).astype(np.float32)


def _rot_cols(w):
    return w[..., _ROT_PERM] * _ROT_SIGN


def _arrange_w_in(w_in):
    o = np.cumsum([0, CONV_W, GDN_VAL_W, 2 * GDN_HEADS, 2 * GDN_HEADS, Q_LORA, KV_LORA, QK_ROPE])
    qkv, z, a, b, cq, ckv, kr = (w_in[:, o[i]:o[i + 1]] for i in range(7))
    misc = jnp.concatenate([a, b, jnp.zeros((w_in.shape[0], LANES - 4 * GDN_HEADS - 2 * QK_ROPE), w_in.dtype),
                            kr, _rot_cols(kr)], axis=1)
    return jnp.concatenate([qkv, z, cq, ckv, misc], axis=1).astype(BF16)


def _arrange_w_uq(w_uq):
    w = w_uq.reshape(Q_LORA, MLA_HEADS, QK_NOPE + QK_ROPE)
    pe = w[:, :, QK_NOPE:]
    return jnp.concatenate([w[:, :, :QK_NOPE], pe, _rot_cols(pe)], axis=-1).reshape(Q_LORA, ATT_W).astype(BF16)


def _arrange_w_ukv(w_ukv):
    w = w_ukv.reshape(KV_LORA, MLA_HEADS, QK_NOPE + V_DIM)
    zk = jnp.zeros((KV_LORA, MLA_HEADS, HEAD_SLOT - QK_NOPE), w.dtype)
    zv = jnp.zeros((KV_LORA, MLA_HEADS, HEAD_SLOT - V_DIM), w.dtype)
    wk = jnp.concatenate([w[:, :, :QK_NOPE], zk], axis=-1).reshape(KV_LORA, ATT_W)
    wv = jnp.concatenate([w[:, :, QK_NOPE:], zv], axis=-1).reshape(KV_LORA, ATT_W)
    return jnp.concatenate([wk, wv], axis=1).astype(BF16)


def _rope_tables(t_lat, t_ctx):
    rows = t_lat // GRID_W
    row = jnp.repeat(jnp.arange(rows), GRID_W).astype(F32)
    col = jnp.tile(jnp.arange(GRID_W), rows).astype(F32)
    inv_freq = ROPE_THETA ** (-jnp.arange(0, AXIS_DIM, 2, dtype=F32) / AXIS_DIM)

    def axis_angles(pos):
        ang = pos[:, None] * inv_freq[None, :]
        return jnp.concatenate([ang, ang], axis=-1)

    ang = jnp.concatenate([axis_angles(row), axis_angles(col)], axis=-1)
    cos, sin = jnp.cos(ang), jnp.sin(ang)
    scale = (QK_NOPE + QK_ROPE) ** -0.5

    def build(cos_t, sin_t):
        t = cos_t.shape[0]
        z32 = jnp.zeros((t, QK_ROPE), F32)
        z64 = jnp.zeros((t, QK_NOPE), F32)
        one64 = jnp.ones((t, QK_NOPE), F32)
        cosq = jnp.concatenate([one64, cos_t, z32], axis=1) * scale
        sinq = jnp.concatenate([z64, sin_t, z32], axis=1) * scale
        cosk = jnp.concatenate([z64, cos_t, z32], axis=1)
        sink = jnp.concatenate([z64, sin_t, z32], axis=1)
        return jnp.concatenate([cosq, sinq, cosk, sink], axis=1)

    lat = build(cos, sin)
    ctx = build(jnp.ones((t_ctx, QK_ROPE), F32), jnp.zeros((t_ctx, QK_ROPE), F32))
    return lat, ctx


def kernel(x, c, ctx, c_ctx, w_mod, b_mod, norm1_g, norm2_g, w_in, conv_w, a_log, dt_bias, gdn_norm_g,
           q_norm_g, w_uq, kv_norm_g, w_ukv, w_out, w_gate, w_up, w_down, final_norm_g):
    batch, seq, d = x.shape
    tctx = ctx.shape[1]
    tab_l, tab_c = _rope_tables(seq, tctx)
    vone = jnp.zeros((MLA_HEADS, HEAD_SLOT), F32).at[:, V_DIM].set(1.0).reshape(1, ATT_W)
    cc = jnp.concatenate([c, c_ctx[None, :], jnp.zeros((16 - batch - 1, d), F32)], axis=0)
    xs = x.reshape(batch * seq, d)
    cs = ctx.reshape(batch * tctx, d)
    tm_l, tm_c, th = 256, 256, 1408

    for l in range(DEPTH):
        last = l == DEPTH - 1
        mod = _modulation(cc, w_mod[l].astype(BF16), b_mod[l][None, :])
        parts = [mod[:, i * d:(i + 1) * d] for i in range(6)]
        lat = [p[:batch].reshape(batch, 1, d) for p in parts]
        cxt = [p[batch:batch + 1].reshape(1, 1, d) for p in parts]
        win = _arrange_w_in(w_in[l])
        wq = _arrange_w_uq(w_uq[l])
        wkv = _arrange_w_ukv(w_ukv[l])
        n1 = norm1_g[l][None, :]
        qg = q_norm_g[l][None, :]
        kvg = kv_norm_g[l][None, :]

        qkv_c, z_c, misc_c, q_c, k_c, v_c = _in_proj(cs, cxt[0], cxt[1], n1, win, qg, wq, kvg, wkv, vone,
                                                     tab_c, tctx, tm_c)
        qkv_l, z_l, misc_l, q_l, k_l, v_l = _in_proj(xs, lat[0], lat[1], n1, win, qg, wq, kvg, wkv, vone,
                                                     tab_l, seq, tm_l)
        yg_c, yg_l = _gdn(qkv_c, misc_c, z_c, qkv_l, misc_l, z_l, conv_w[l],
                          a_log[l].reshape(1, -1), dt_bias[l].reshape(1, -1), gdn_norm_g[l][None, :],
                          batch, tctx, seq)
        ym_l = _attention(q_l, (k_c, k_l), (v_c, v_l), batch, seq, 512, (tctx, seq))

        wo = w_out[l].astype(BF16)
        wg, wu, wd = w_gate[l].astype(BF16), w_up[l].astype(BF16), w_down[l].astype(BF16)
        n2 = norm2_g[l][None, :]
        fn = final_norm_g[None, :]
        xs = _out_ffn(xs, yg_l, ym_l, wo, lat[2], lat[3], lat[4], lat[5], n2, wg, wu, wd, fn,
                      seq, tm_l * 2, th, last)
        if not last:
            ym_c = _attention(q_c, (k_c,), (v_c,), batch, tctx, tctx, (tctx,))
            cs = _out_ffn(cs, yg_c, ym_c, wo, cxt[2], cxt[3], cxt[4], cxt[5], n2, wg, wu, wd, fn,
                          tctx, tm_c, th, False)
    return xs.reshape(batch, seq, d)
```

```python
import functools
import math

import jax
import jax.numpy as jnp
import numpy as np
from jax import lax
from jax.experimental import pallas as pl
from jax.experimental.pallas import tpu as pltpu

D_MODEL = 1024
DEPTH = 2
GRID_W = 64
EPS = 1e-6

GDN_HEADS = 8
GDN_DK = 64
GDN_DV = 64
GDN_KEY_W = GDN_HEADS * GDN_DK
GDN_VAL_W = GDN_HEADS * GDN_DV
CONV_W = 2 * GDN_KEY_W + GDN_VAL_W
CONV_K = 5
CHUNK = 64
GDN_SUB = 8

MLA_HEADS = 8
QK_NOPE = 64
QK_ROPE = 32
V_DIM = 64
Q_LORA = 256
KV_LORA = 128
ROPE_THETA = 10000.0
AXIS_DIM = QK_ROPE // 2
FFN_HIDDEN = -(-8 * D_MODEL // (3 * 256)) * 256

LANES = 128
SUBLANES = 8
HEAD_SLOT = LANES
ATT_W = MLA_HEADS * HEAD_SLOT
IN_W = CONV_W + GDN_VAL_W + Q_LORA + KV_LORA + LANES
VMEM_LIMIT = 56 * 1024 * 1024

F32 = jnp.float32
BF16 = jnp.bfloat16


def _sigmoid(x):
    return 0.5 * (1.0 + jnp.tanh(0.5 * x))


def _silu(x):
    return x * _sigmoid(x)


def _rms(x, g):
    return x * lax.rsqrt(jnp.mean(x * x, axis=-1, keepdims=True) + EPS) * g


def _mm(a, b):
    return jnp.dot(a, b, preferred_element_type=F32)


def _mod_kernel(c_ref, w_ref, b_ref, o_ref):
    o_ref[...] = _mm(_silu(c_ref[...]).astype(BF16), w_ref[...]) + b_ref[...]


def _modulation(cc, w_mod, b_mod):
    rows, d = cc.shape
    n = w_mod.shape[1]
    tn = 1024
    return pl.pallas_call(
        _mod_kernel,
        grid=(n // tn,),
        in_specs=[pl.BlockSpec((rows, d), lambda j: (0, 0)),
                  pl.BlockSpec((d, tn), lambda j: (0, j)),
                  pl.BlockSpec((1, tn), lambda j: (0, j))],
        out_specs=pl.BlockSpec((rows, tn), lambda j: (0, j)),
        out_shape=jax.ShapeDtypeStruct((rows, n), F32),
        compiler_params=pltpu.CompilerParams(dimension_semantics=("arbitrary",),
                                             vmem_limit_bytes=VMEM_LIMIT),
        name="modulation",
    )(cc, w_mod, b_mod)


def _in_kernel(x_ref, sh_ref, sc_ref, g_ref, win_ref, qg_ref, wq_ref, kvg_ref, wkv_ref, vone_ref, tab_ref,
               qkv_ref, z_ref, misc_ref, q_ref, k_ref, v_ref):
    x = x_ref[...]
    h = _rms(x, g_ref[...]) * (1.0 + sc_ref[0]) + sh_ref[0]
    p = _mm(h.astype(BF16), win_ref[...])
    o1 = CONV_W
    o2 = o1 + GDN_VAL_W
    o3 = o2 + Q_LORA
    o4 = o3 + KV_LORA
    qkv_ref[...] = p[:, :o1]
    z_ref[...] = p[:, o1:o2]
    misc = p[:, o4:]
    misc_ref[...] = misc
    cqn = _rms(p[:, o2:o3], qg_ref[...]).astype(BF16)
    ckvn = _rms(p[:, o3:o4], kvg_ref[...]).astype(BF16)
    qa = _mm(cqn, wq_ref[...])
    kv = _mm(ckvn, wkv_ref[...])
    tab = tab_ref[...]
    cosq, sinq = tab[:, 0:LANES], tab[:, LANES:2 * LANES]
    cosk, sink = tab[:, 2 * LANES:3 * LANES], tab[:, 3 * LANES:4 * LANES]
    kpe = misc * cosk + pltpu.roll(misc, LANES - QK_ROPE, 1) * sink
    for hd in range(MLA_HEADS):
        sl = slice(hd * HEAD_SLOT, (hd + 1) * HEAD_SLOT)
        qh = qa[:, sl]
        q_ref[:, sl] = (qh * cosq + pltpu.roll(qh, LANES - QK_ROPE, 1) * sinq).astype(BF16)
        k_ref[:, sl] = (kv[:, sl] + kpe).astype(BF16)
    v_ref[...] = (kv[:, ATT_W:] + vone_ref[...]).astype(BF16)


def _in_proj(x, sh, sc, g, win, qg, wq, kvg, wkv, vone, tab, seq, tm):
    n, d = x.shape
    per_b = seq // tm
    per_batch_mod = sh.shape[0] > 1
    mod_map = (lambda i: (i // per_b, 0, 0)) if per_batch_mod else (lambda i: (0, 0, 0))
    const = lambda i: (0, 0)
    row = lambda i: (i, 0)
    out_shapes = (jax.ShapeDtypeStruct((n, CONV_W), F32), jax.ShapeDtypeStruct((n, GDN_VAL_W), F32),
                  jax.ShapeDtypeStruct((n, LANES), F32), jax.ShapeDtypeStruct((n, ATT_W), BF16),
                  jax.ShapeDtypeStruct((n, ATT_W), BF16), jax.ShapeDtypeStruct((n, ATT_W), BF16))
    return pl.pallas_call(
        _in_kernel,
        grid=(n // tm,),
        in_specs=[pl.BlockSpec((tm, d), row),
                  pl.BlockSpec((1, 1, d), mod_map), pl.BlockSpec((1, 1, d), mod_map),
                  pl.BlockSpec((1, d), const), pl.BlockSpec((d, IN_W), const),
                  pl.BlockSpec((1, Q_LORA), const), pl.BlockSpec((Q_LORA, ATT_W), const),
                  pl.BlockSpec((1, KV_LORA), const), pl.BlockSpec((KV_LORA, 2 * ATT_W), const),
                  pl.BlockSpec((1, ATT_W), const),
                  pl.BlockSpec((tm, 4 * LANES), lambda i: (i % per_b, 0))],
        out_specs=[pl.BlockSpec((tm, CONV_W), row), pl.BlockSpec((tm, GDN_VAL_W), row),
                   pl.BlockSpec((tm, LANES), row), pl.BlockSpec((tm, ATT_W), row),
                   pl.BlockSpec((tm, ATT_W), row), pl.BlockSpec((tm, ATT_W), row)],
        out_shape=out_shapes,
        compiler_params=pltpu.CompilerParams(dimension_semantics=("arbitrary",),
                                             vmem_limit_bytes=VMEM_LIMIT),
        name="in_proj",
    )(x, sh, sc, g, win, qg, wq, kvg, wkv, vone, tab)


def _conv_silu(x_ref, w_ref, pad_s):
    t = x_ref.shape[0]
    zero = jnp.zeros((SUBLANES, LANES), F32)
    pad_s[0:SUBLANES, :] = zero
    pad_s[SUBLANES + t:2 * SUBLANES + t, :] = zero
    pad_s[SUBLANES:SUBLANES + t, :] = x_ref[...]
    acc = None
    for j in range(CONV_K):
        off = SUBLANES - CONV_K // 2 + j
        term = pad_s[off:off + t, :] * w_ref[j:j + 1, :]
        acc = term if acc is None else acc + term
    return _silu(acc)


def _hi_lo(x):
    hi = x.astype(BF16).astype(F32)
    return hi, (x - hi).astype(BF16).astype(F32)


def _split3(x, axis):
    hi = x.astype(BF16).astype(F32)
    mid = (x - hi).astype(BF16).astype(F32)
    return jnp.concatenate([hi, mid, x - hi - mid], axis=axis).astype(BF16)


def _seg_sum(x, gmat2):
    hi, lo = _hi_lo(x)
    return _mm(jnp.concatenate([hi, lo], axis=1).astype(BF16), gmat2)


def _bd_rows(x, left):
    return jnp.concatenate([jnp.where(left, x, 0.0), jnp.where(left, 0.0, x)], axis=0)


def _bd_w(x, left):
    return _bd_rows(x, left).astype(BF16)


def _tri_inverse_pair(amats, lvl_mask, eye, left):
    c = amats[0].shape[0]
    n1 = [jnp.where(lvl_mask[0], -a, 0.0) for a in amats]
    n2 = [_mm(n.astype(BF16), _bd_w(n, left)) for n in n1]
    t = [eye + n for n in n1]
    y = [_mm(jnp.concatenate([x, n], axis=0).astype(BF16), _bd_w(n, left)) for x, n in zip(t, n2)]
    t = [x + yy[0:c] for x, yy in zip(t, y)]
    t = [x + _mm(x.astype(BF16), _bd_w(yy[c:2 * c], left)) for x, yy in zip(t, y)]
    for m in lvl_mask[1:]:
        tq = [_mm(x.astype(BF16), _bd_w(jnp.where(m, a, 0.0), left)) for x, a in zip(t, amats)]
        t = [x - _mm(y_.astype(BF16), _bd_w(x, left)) for x, y_ in zip(t, tq)]
    return t


def _gdn_kernel(qc_ref, kc_ref, vc_ref, mc_ref, zc_ref, ql_ref, kl_ref, vl_ref, ml_ref, zl_ref,
                wq_ref, wk_ref, wv_ref, alog_ref, dtb_ref, gn_ref,
                yc_ref, yl_ref,
                q_s, k_s, v_s, gate_s, gcc_s, bb_s, acc_s, pad_s):
    pair = pl.program_id(1)
    c = CHUNK
    tc = qc_ref.shape[0]
    tl = ql_ref.shape[0]
    tt = tc + tl
    ngate = 2 * GDN_HEADS
    ri = lax.broadcasted_iota(jnp.int32, (c, LANES), 0)
    li = lax.broadcasted_iota(jnp.int32, (c, LANES), 1)
    ci = li & (c - 1)
    left = li < c
    incl = (ri >= ci, ri <= ci)
    strict = (ri > ci, ri < ci)
    eye = (ri == ci).astype(F32)
    bi, bj = ri >> 3, ci >> 3
    lvl_mask = (bi == bj,
                ((bi >> 1) == (bj >> 1)) & (bi != bj),
                ((bi >> 2) == (bj >> 2)) & ((bi >> 1) != (bj >> 1)),
                (bi >> 2) != (bj >> 2))
    rr = lax.broadcasted_iota(jnp.int32, (LANES, LANES), 0)
    ll = lax.broadcasted_iota(jnp.int32, (LANES, LANES), 1)
    bd = (rr >> 6) == (ll >> 6)
    top = rr < c
    gmat = bd.astype(BF16)
    gmat2 = jnp.concatenate([gmat, gmat], axis=0)
    cum_p = (bd & (ll <= rr)).astype(BF16)
    cum_s = (bd & (ll >= rr)).astype(BF16)
    cum3 = (jnp.concatenate([cum_p] * 3, axis=1), jnp.concatenate([cum_s] * 3, axis=1))

    def stage(q_ref, k_ref, v_ref, m_ref, base, t):
        uq = _conv_silu(q_ref, wq_ref, pad_s)
        q_s[base:base + t, :] = uq * lax.rsqrt(_seg_sum(uq * uq, gmat2) + EPS) * (GDN_DK ** -0.5)
        uk = _conv_silu(k_ref, wk_ref, pad_s)
        k_s[base:base + t, :] = uk * lax.rsqrt(_seg_sum(uk * uk, gmat2) + EPS)
        v_s[base:base + t, :] = _conv_silu(v_ref, wv_ref, pad_s)
        m = m_ref[...]
        a = m[:, 0:ngate] + dtb_ref[...]
        sp = jnp.maximum(a, 0.0) + jnp.log(1.0 + jnp.exp(-jnp.abs(a)))
        gate_s[base:base + t, 0:ngate] = -jnp.exp(alog_ref[...]) * sp
        gate_s[base:base + t, ngate:2 * ngate] = _sigmoid(m[:, ngate:2 * ngate])

    gate_s[...] = jnp.zeros_like(gate_s)
    acc_s[...] = jnp.zeros_like(acc_s)
    stage(qc_ref, kc_ref, vc_ref, mc_ref, 0, tc)
    stage(ql_ref, kl_ref, vl_ref, ml_ref, tc, tl)

    src = lax.broadcasted_iota(jnp.int32, (3 * LANES, 4 * LANES), 0) & (LANES - 1)
    col = lax.broadcasted_iota(jnp.int32, (3 * LANES, 4 * LANES), 1)
    quantity, hh = col >> 7, (col >> 6) & 1
    want = (quantity & 1) * GDN_HEADS + (quantity >> 1) * ngate + 2 * pair + hh
    expand = (src == want).astype(BF16)
    out = _mm(_split3(gate_s[...], 1), expand)
    for d in range(2):
        gcc_s[d] = out[:, d * LANES:(d + 1) * LANES]
        bb_s[d] = out[:, (2 + d) * LANES:(3 + d) * LANES]

    def cumsum_tiles(i, carry):
        rows = pl.ds(pl.multiple_of(i * LANES, LANES), LANES)
        for d in range(2):
            gcc_s[d, rows, :] = _mm(cum3[d], _split3(gcc_s[d, rows, :], 0))
        return carry

    lax.fori_loop(0, tt // LANES, cumsum_tiles, 0, unroll=2)

    def chunk_step(streams, starts, states):
        n = range(len(streams))
        dirs = [d for d, u in streams]
        rows = [pl.ds(s, c) for s in starts]
        q = [q_s[r, :] for r in rows]
        k = [k_s[r, :] for r in rows]
        v = [v_s[r, :] for r in rows]
        gcc = [gcc_s[d, r, :] for d, r in zip(dirs, rows)]
        bb = [bb_s[d, r, :] for d, r in zip(dirs, rows)]
        grow = [jnp.sum(eye * g, axis=0, keepdims=True) for g in gcc]
        decay = [jnp.where(incl[d], jnp.exp(jnp.where(incl[d], gcc[i] - grow[i], 0.0)), 0.0)
                 for i, d in enumerate(dirs)]
        kb = [k[i] * bb[i] for i in n]
        kt = [jnp.transpose(k[i]) for i in n]
        kbd = [jnp.where(bd, jnp.concatenate([kt[i], kt[i]], axis=1), 0.0).astype(BF16) for i in n]
        pm = [_mm(jnp.concatenate([kb[i], q[i]], axis=0).astype(BF16), kbd[i]) for i in n]
        amat = [jnp.where(strict[d], pm[i][0:c] * decay[i], 0.0) for i, d in enumerate(dirs)]
        qk = [(pm[i][c:2 * c] * decay[i]).astype(BF16) for i in n]
        tinv = _tri_inverse_pair(amat, lvl_mask, eye, left)
        egc = [jnp.exp(g) for g in gcc]
        gl = [gcc[i][c - 1:c, :] if d == 0 else gcc[i][0:1, :] for i, d in enumerate(dirs)]
        rhs = [jnp.concatenate([_bd_rows(kb[i] * egc[i], left), _bd_rows(v[i] * bb[i], left)], axis=1).astype(BF16)
               for i in n]
        wu = [_mm(tinv[i].astype(BF16), rhs[i]) for i in n]
        wub = [x.astype(BF16) for x in wu]
        wbd = [jnp.concatenate([_bd_rows(wu[i][:, 0:LANES], left), _bd_rows(wu[i][:, LANES:], left)],
                               axis=1).astype(BF16) for i in n]
        kscale = [jnp.exp(gl[i] - grow[i]) for i in n]
        kgt = [(kt[i] * jnp.where(top[:, 0:c], kscale[i][:, 0:c], kscale[i][:, c:])).astype(BF16) for i in n]
        qkwu = [_mm(qk[i], wbd[i]) for i in n]
        kgwu = [_mm(kgt[i], wub[i]) for i in n]
        lhs = [jnp.concatenate([jnp.where(bd, -kgwu[i][:, 0:LANES], 0.0),
                                q[i] * egc[i] - qkwu[i][:, 0:LANES]], axis=0).astype(BF16) for i in n]
        nmat = [jnp.where(bd, kgwu[i][:, LANES:], 0.0) for i in n]
        egl = [jnp.exp(g) for g in gl]
        gsc = [jnp.where(top, egl[i][:, 0:1], egl[i][:, c:c + 1]) for i in n]
        st = list(states)
        for u in sorted({u for d, u in streams}):
            sel = [i for i in n if streams[i][1] == u]
            mqs = {i: _mm(lhs[i], st[dirs[i]].astype(BF16)) for i in sel}
            for i in sel:
                d = dirs[i]
                st[d] = st[d] * gsc[i] + mqs[i][0:LANES] + nmat[i]
                acc_s[rows[i], :] = acc_s[rows[i], :] + qkwu[i][:, LANES:] + mqs[i][LANES:]
        return tuple(st)

    def run(base, nchunks, nsub, states):
        streams = [(d, u) for u in range(nsub) for d in range(2)]

        def body(i, st):
            pos = [i * nsub + u for d, u in streams]
            idx = [p if d == 0 else nchunks - 1 - p for p, (d, u) in zip(pos, streams)]
            return chunk_step(streams, [pl.multiple_of(base + x * c, c) for x in idx], st)
        return lax.fori_loop(0, nchunks // nsub, body, states)

    zero = jnp.zeros((LANES, LANES), F32)
    states = run(0, tc // c, math.gcd(tc // c, GDN_SUB), (zero, zero))
    run(tc, tl // c, math.gcd(tl // c, GDN_SUB), states)

    def finish(z_ref, y_ref, base, t):
        o = acc_s[base:base + t, :]
        ms = _seg_sum(o * o, gmat2) * (1.0 / GDN_DV)
        y_ref[...] = (o * lax.rsqrt(ms + EPS) * gn_ref[...] * _silu(z_ref[...])).astype(y_ref.dtype)

    finish(zc_ref, yc_ref, 0, tc)
    finish(zl_ref, yl_ref, tc, tl)


def _gdn(qkv_c, misc_c, z_c, qkv_l, misc_l, z_l, conv_w, alog16, dtb16, gn2, batch, tc, tl):
    npair = GDN_HEADS // 2
    kq, kk, kvv = 0, npair, 2 * npair
    blk = lambda t, off: pl.BlockSpec((t, LANES), lambda b, p: (b, off + p))
    mblk = lambda t: pl.BlockSpec((t, LANES), lambda b, p: (b, 0))
    wblk = lambda off: pl.BlockSpec((CONV_K, LANES), lambda b, p: (0, off + p))
    const = lambda b, p: (0, 0)
    tt = tc + tl
    seq_buf = pltpu.VMEM((tt, LANES), F32)
    return pl.pallas_call(
        _gdn_kernel,
        grid=(batch, npair),
        in_specs=[blk(tc, kq), blk(tc, kk), blk(tc, kvv), mblk(tc), blk(tc, 0),
                  blk(tl, kq), blk(tl, kk), blk(tl, kvv), mblk(tl), blk(tl, 0),
                  wblk(kq), wblk(kk), wblk(kvv),
                  pl.BlockSpec((1, 2 * GDN_HEADS), const), pl.BlockSpec((1, 2 * GDN_HEADS), const),
                  pl.BlockSpec((1, LANES), const)],
        out_specs=[blk(tc, 0), blk(tl, 0)],
        out_shape=(jax.ShapeDtypeStruct((batch * tc, GDN_VAL_W), BF16),
                   jax.ShapeDtypeStruct((batch * tl, GDN_VAL_W), BF16)),
        scratch_shapes=[seq_buf, seq_buf, seq_buf, seq_buf,
                        pltpu.VMEM((2, tt, LANES), F32), pltpu.VMEM((2, tt, LANES), F32), seq_buf,
                        pltpu.VMEM((max(tc, tl) + 2 * SUBLANES, LANES), F32)],
        compiler_params=pltpu.CompilerParams(dimension_semantics=("arbitrary", "arbitrary"),
                                             vmem_limit_bytes=VMEM_LIMIT),
        name="gdn",
    )(qkv_c, qkv_c, qkv_c, misc_c, z_c, qkv_l, qkv_l, qkv_l, misc_l, z_l,
      conv_w, conv_w, conv_w, alog16, dtb16, gn2)


def _attn_kernel(*refs, n_src):
    q_ref = refs[0]
    k_refs = refs[1:1 + n_src]
    v_refs = refs[1 + n_src:1 + 2 * n_src]
    o_ref = refs[1 + 2 * n_src]
    heads = range(2)
    sls = [slice(j * HEAD_SLOT, (j + 1) * HEAD_SLOT) for j in heads]
    scores = [[lax.dot_general(q_ref[:, sl], kr[:, sl], (((1,), (1,)), ((), ())), preferred_element_type=F32)
               for kr in k_refs] for sl in sls]
    for j in heads:
        m = None
        for s in scores[j]:
            ms = jnp.max(s, axis=-1, keepdims=True)
            m = ms if m is None else jnp.maximum(m, ms)
        acc = None
        for s, vr in zip(scores[j], v_refs):
            p = jnp.exp(s - m).astype(BF16)
            t = _mm(p, vr[:, sls[j]])
            acc = t if acc is None else acc + t
        out = acc[:, 0:V_DIM] / acc[:, V_DIM:V_DIM + 1]
        o_ref[:, j * V_DIM:(j + 1) * V_DIM] = out.astype(o_ref.dtype)


def _attention(q, ks, vs, batch, tq_total, tq, src_lens):
    n_src = len(ks)
    nq = tq_total // tq
    npair = MLA_HEADS // 2
    qmap = lambda b, p, i: (b * nq + i, p)
    kvmap = lambda b, p, i: (b, p)
    in_specs = [pl.BlockSpec((tq, 2 * HEAD_SLOT), qmap)]
    in_specs += [pl.BlockSpec((t, 2 * HEAD_SLOT), kvmap) for t in src_lens]
    in_specs += [pl.BlockSpec((t, 2 * HEAD_SLOT), kvmap) for t in src_lens]
    return pl.pallas_call(
        functools.partial(_attn_kernel, n_src=n_src),
        grid=(batch, npair, nq),
        in_specs=in_specs,
        out_specs=pl.BlockSpec((tq, 2 * V_DIM), qmap),
        out_shape=jax.ShapeDtypeStruct((batch * tq_total, MLA_HEADS * V_DIM), BF16),
        compiler_params=pltpu.CompilerParams(dimension_semantics=("arbitrary", "arbitrary", "arbitrary"),
                                             vmem_limit_bytes=VMEM_LIMIT),
        name="attention",
    )(q, *ks, *vs)


def _out_ffn_kernel(x_ref, yg_ref, ym_ref, wo_ref, g1_ref, sh2_ref, sc2_ref, g2_ref, n2_ref,
                    wg_ref, wu_ref, wd_ref, fn_ref, o_ref, x1_s, h2_s, acc_s, *, final_norm):
    j = pl.program_id(1)

    @pl.when(j == 0)
    def _():
        half = GDN_VAL_W
        y = _mm(yg_ref[...], wo_ref[0:half, :]) + _mm(ym_ref[...], wo_ref[half:, :])
        x1 = x_ref[...] + g1_ref[0] * y
        x1_s[...] = x1
        h2 = _rms(x1, n2_ref[...]) * (1.0 + sc2_ref[0]) + sh2_ref[0]
        h2_s[...] = h2.astype(BF16)
        acc_s[...] = jnp.zeros_like(acc_s)

    h2 = h2_s[...]
    act = (_silu(_mm(h2, wg_ref[...])) * _mm(h2, wu_ref[...])).astype(BF16)
    acc_s[...] += _mm(act, wd_ref[...])

    @pl.when(j == pl.num_programs(1) - 1)
    def _():
        x2 = x1_s[...] + g2_ref[0] * acc_s[...]
        if final_norm:
            x2 = _rms(x2, fn_ref[...])
        o_ref[...] = x2


def _out_ffn(x, yg, ym, wo, g1, sh2, sc2, g2, n2, wg, wu, wd, fn, seq, tm, th, final_norm):
    n, d = x.shape
    per_b = seq // tm
    per_batch_mod = g1.shape[0] > 1
    mod_map = (lambda i, j: (i // per_b, 0, 0)) if per_batch_mod else (lambda i, j: (0, 0, 0))
    const = lambda i, j: (0, 0)
    row = lambda i, j: (i, 0)
    hid = FFN_HIDDEN
    mod_spec = pl.BlockSpec((1, 1, d), mod_map)
    return pl.pallas_call(
        functools.partial(_out_ffn_kernel, final_norm=final_norm),
        grid=(n // tm, hid // th),
        in_specs=[pl.BlockSpec((tm, d), row),
                  pl.BlockSpec((tm, GDN_VAL_W), row), pl.BlockSpec((tm, MLA_HEADS * V_DIM), row),
                  pl.BlockSpec((d, d), const),
                  mod_spec, mod_spec, mod_spec, mod_spec,
                  pl.BlockSpec((1, d), const),
                  pl.BlockSpec((d, th), lambda i, j: (0, j)), pl.BlockSpec((d, th), lambda i, j: (0, j)),
                  pl.BlockSpec((th, d), lambda i, j: (j, 0)),
                  pl.BlockSpec((1, d), const)],
        out_specs=pl.BlockSpec((tm, d), row),
        out_shape=jax.ShapeDtypeStruct((n, d), F32),
        scratch_shapes=[pltpu.VMEM((tm, d), F32), pltpu.VMEM((tm, d), BF16), pltpu.VMEM((tm, d), F32)],
        compiler_params=pltpu.CompilerParams(dimension_semantics=("arbitrary", "arbitrary"),
                                             vmem_limit_bytes=VMEM_LIMIT),
        name="out_ffn",
    )(x, yg, ym, wo, g1, sh2, sc2, g2, n2, wg, wu, wd, fn)


_ROT_PERM = np.concatenate([np.arange(8, 16), np.arange(0, 8), np.arange(24, 32), np.arange(16, 24)])
_ROT_SIGN = np.concatenate([-np.ones(8), np.ones(8), -np.ones(8), np.ones(8)]).astype(np.float32)


def _rot_cols(w):
    return w[..., _ROT_PERM] * _ROT_SIGN


def _arrange_w_in(w_in):
    o = np.cumsum([0, CONV_W, GDN_VAL_W, 2 * GDN_HEADS, 2 * GDN_HEADS, Q_LORA, KV_LORA, QK_ROPE])
    qkv, z, a, b, cq, ckv, kr = (w_in[:, o[i]:o[i + 1]] for i in range(7))
    misc = jnp.concatenate([a, b, jnp.zeros((w_in.shape[0], LANES - 4 * GDN_HEADS - 2 * QK_ROPE), w_in.dtype),
                            kr, _rot_cols(kr)], axis=1)
    return jnp.concatenate([qkv, z, cq, ckv, misc], axis=1).astype(BF16)


def _arrange_w_uq(w_uq):
    w = w_uq.reshape(Q_LORA, MLA_HEADS, QK_NOPE + QK_ROPE)
    pe = w[:, :, QK_NOPE:]
    return jnp.concatenate([w[:, :, :QK_NOPE], pe, _rot_cols(pe)], axis=-1).reshape(Q_LORA, ATT_W).astype(BF16)


def _arrange_w_ukv(w_ukv):
    w = w_ukv.reshape(KV_LORA, MLA_HEADS, QK_NOPE + V_DIM)
    zk = jnp.zeros((KV_LORA, MLA_HEADS, HEAD_SLOT - QK_NOPE), w.dtype)
    zv = jnp.zeros((KV_LORA, MLA_HEADS, HEAD_SLOT - V_DIM), w.dtype)
    wk = jnp.concatenate([w[:, :, :QK_NOPE], zk], axis=-1).reshape(KV_LORA, ATT_W)
    wv = jnp.concatenate([w[:, :, QK_NOPE:], zv], axis=-1).reshape(KV_LORA, ATT_W)
    return jnp.concatenate([wk, wv], axis=1).astype(BF16)


def _rope_tables(t_lat, t_ctx):
    rows = t_lat // GRID_W
    row = jnp.repeat(jnp.arange(rows), GRID_W).astype(F32)
    col = jnp.tile(jnp.arange(GRID_W), rows).astype(F32)
    inv_freq = ROPE_THETA ** (-jnp.arange(0, AXIS_DIM, 2, dtype=F32) / AXIS_DIM)

    def axis_angles(pos):
        ang = pos[:, None] * inv_freq[None, :]
        return jnp.concatenate([ang, ang], axis=-1)

    ang = jnp.concatenate([axis_angles(row), axis_angles(col)], axis=-1)
    cos, sin = jnp.cos(ang), jnp.sin(ang)
    scale = (QK_NOPE + QK_ROPE) ** -0.5

    def build(cos_t, sin_t):
        t = cos_t.shape[0]
        z32 = jnp.zeros((t, QK_ROPE), F32)
        z64 = jnp.zeros((t, QK_NOPE), F32)
        one64 = jnp.ones((t, QK_NOPE), F32)
        cosq = jnp.concatenate([one64, cos_t, z32], axis=1) * scale
        sinq = jnp.concatenate([z64, sin_t, z32], axis=1) * scale
        cosk = jnp.concatenate([z64, cos_t, z32], axis=1)
        sink = jnp.concatenate([z64, sin_t, z32], axis=1)
        return jnp.concatenate([cosq, sinq, cosk, sink], axis=1)

    lat = build(cos, sin)
    ctx = build(jnp.ones((t_ctx, QK_ROPE), F32), jnp.zeros((t_ctx, QK_ROPE), F32))
    return lat, ctx


def kernel(x, c, ctx, c_ctx, w_mod, b_mod, norm1_g, norm2_g, w_in, conv_w, a_log, dt_bias, gdn_norm_g,
           q_norm_g, w_uq, kv_norm_g, w_ukv, w_out, w_gate, w_up, w_down, final_norm_g):
    batch, seq, d = x.shape
    tctx = ctx.shape[1]
    tab_l, tab_c = _rope_tables(seq, tctx)
    vone = jnp.zeros((MLA_HEADS, HEAD_SLOT), F32).at[:, V_DIM].set(1.0).reshape(1, ATT_W)
    cc = jnp.concatenate([c, c_ctx[None, :], jnp.zeros((16 - batch - 1, d), F32)], axis=0)
    xs = x.reshape(batch * seq, d)
    cs = ctx.reshape(batch * tctx, d)
    tm_l, tm_c, th = 256, 256, 1408

    for l in range(DEPTH):
        last = l == DEPTH - 1
        mod = _modulation(cc, w_mod[l].astype(BF16), b_mod[l][None, :])
        parts = [mod[:, i * d:(i + 1) * d] for i in range(6)]
        lat = [p[:batch].reshape(batch, 1, d) for p in parts]
        cxt = [p[batch:batch + 1].reshape(1, 1, d) for p in parts]
        win = _arrange_w_in(w_in[l])
        wq = _arrange_w_uq(w_uq[l])
        wkv = _arrange_w_ukv(w_ukv[l])
        n1 = norm1_g[l][None, :]
        qg = q_norm_g[l][None, :]
        kvg = kv_norm_g[l][None, :]

        qkv_c, z_c, misc_c, q_c, k_c, v_c = _in_proj(cs, cxt[0], cxt[1], n1, win, qg, wq, kvg, wkv, vone,
                                                     tab_c, tctx, tm_c)
        qkv_l, z_l, misc_l, q_l, k_l, v_l = _in_proj(xs, lat[0], lat[1], n1, win, qg, wq, kvg, wkv, vone,
                                                     tab_l, seq, tm_l)
        yg_c, yg_l = _gdn(qkv_c, misc_c, z_c, qkv_l, misc_l, z_l, conv_w[l],
                          a_log[l].reshape(1, -1), dt_bias[l].reshape(1, -1),
                          jnp.tile(gdn_norm_g[l], 2)[None, :], batch, tctx, seq)
        ym_l = _attention(q_l, (k_c, k_l), (v_c, v_l), batch, seq, 512, (tctx, seq))

        wo = w_out[l].astype(BF16)
        wg, wu, wd = w_gate[l].astype(BF16), w_up[l].astype(BF16), w_down[l].astype(BF16)
        n2 = norm2_g[l][None, :]
        fn = final_norm_g[None, :]
        xs = _out_ffn(xs, yg_l, ym_l, wo, lat[2], lat[3], lat[4], lat[5], n2, wg, wu, wd, fn,
                      seq, tm_l * 2, th, last)
        if not last:
            ym_c = _attention(q_c, (k_c,), (v_c,), batch, tctx, tctx, (tctx,))
            cs = _out_ffn(cs, yg_c, ym_c, wo, cxt[2], cxt[3], cxt[4], cxt[5], n2, wg, wu, wd, fn,
                          tctx, tm_c, th, False)
    return xs.reshape(batch, seq, d)
```

```python
import functools
import math

import jax
import jax.numpy as jnp
import numpy as np
from jax import lax
from jax.experimental import pallas as pl
from jax.experimental.pallas import tpu as pltpu

D_MODEL = 1024
DEPTH = 2
GRID_W = 64
EPS = 1e-6

GDN_HEADS = 8
GDN_DK = 64
GDN_DV = 64
GDN_KEY_W = GDN_HEADS * GDN_DK
GDN_VAL_W = GDN_HEADS * GDN_DV
CONV_W = 2 * GDN_KEY_W + GDN_VAL_W
CONV_K = 5
CHUNK = 64
GDN_SUB = 8

MLA_HEADS = 8
QK_NOPE = 64
QK_ROPE = 32
V_DIM = 64
Q_LORA = 256
KV_LORA = 128
ROPE_THETA = 10000.0
AXIS_DIM = QK_ROPE // 2
FFN_HIDDEN = -(-8 * D_MODEL // (3 * 256)) * 256

LANES = 128
SUBLANES = 8
HEAD_SLOT = LANES
ATT_W = MLA_HEADS * HEAD_SLOT
IN_W = CONV_W + GDN_VAL_W + Q_LORA + KV_LORA + LANES
VMEM_LIMIT = 56 * 1024 * 1024

F32 = jnp.float32
BF16 = jnp.bfloat16


def _sigmoid(x):
    return 0.5 * (1.0 + jnp.tanh(0.5 * x))


def _silu(x):
    return x * _sigmoid(x)


def _rms(x, g):
    return x * lax.rsqrt(jnp.mean(x * x, axis=-1, keepdims=True) + EPS) * g


def _mm(a, b):
    return jnp.dot(a, b, preferred_element_type=F32)


def _mod_kernel(c_ref, w_ref, b_ref, o_ref):
    o_ref[...] = _mm(_silu(c_ref[...]).astype(BF16), w_ref[...]) + b_ref[...]


def _modulation(cc, w_mod, b_mod):
    rows, d = cc.shape
    depth, _, n = w_mod.shape
    tn = 1024
    return pl.pallas_call(
        _mod_kernel,
        grid=(depth, n // tn),
        in_specs=[pl.BlockSpec((rows, d), lambda l, j: (0, 0)),
                  pl.BlockSpec((None, d, tn), lambda l, j: (l, 0, j)),
                  pl.BlockSpec((None, 1, tn), lambda l, j: (l, 0, j))],
        out_specs=pl.BlockSpec((None, rows, tn), lambda l, j: (l, 0, j)),
        out_shape=jax.ShapeDtypeStruct((depth, rows, n), F32),
        compiler_params=pltpu.CompilerParams(dimension_semantics=("arbitrary", "arbitrary"),
                                             vmem_limit_bytes=VMEM_LIMIT),
        name="modulation",
    )(cc, w_mod, b_mod)


def _in_kernel(x_ref, sh_ref, sc_ref, g_ref, win_ref, qg_ref, wq_ref, kvg_ref, wkv_ref, vone_ref, tab_ref,
               qkv_ref, z_ref, misc_ref, q_ref, k_ref, v_ref):
    x = x_ref[...]
    h = _rms(x, g_ref[...]) * (1.0 + sc_ref[0]) + sh_ref[0]
    p = _mm(h.astype(BF16), win_ref[...])
    o1 = CONV_W
    o2 = o1 + GDN_VAL_W
    o3 = o2 + Q_LORA
    o4 = o3 + KV_LORA
    qkv_ref[...] = p[:, :o1]
    z_ref[...] = p[:, o1:o2]
    misc = p[:, o4:]
    misc_ref[...] = misc
    cqn = _rms(p[:, o2:o3], qg_ref[...]).astype(BF16)
    ckvn = _rms(p[:, o3:o4], kvg_ref[...]).astype(BF16)
    qa = _mm(cqn, wq_ref[...])
    kv = _mm(ckvn, wkv_ref[...])
    tab = tab_ref[...]
    cosq, sinq = tab[:, 0:LANES], tab[:, LANES:2 * LANES]
    cosk, sink = tab[:, 2 * LANES:3 * LANES], tab[:, 3 * LANES:4 * LANES]
    kpe = misc * cosk + pltpu.roll(misc, LANES - QK_ROPE, 1) * sink
    for hd in range(MLA_HEADS):
        sl = slice(hd * HEAD_SLOT, (hd + 1) * HEAD_SLOT)
        qh = qa[:, sl]
        q_ref[:, sl] = (qh * cosq + pltpu.roll(qh, LANES - QK_ROPE, 1) * sinq).astype(BF16)
        k_ref[:, sl] = (kv[:, sl] + kpe).astype(BF16)
    v_ref[...] = (kv[:, ATT_W:] + vone_ref[...]).astype(BF16)


def _in_proj(x, sh, sc, g, win, qg, wq, kvg, wkv, vone, tab, seq, tm, layer):
    n, d = x.shape
    per_b = seq // tm
    per_batch_mod = sh.shape[0] > 1
    mod_map = (lambda i: (i // per_b, 0, 0)) if per_batch_mod else (lambda i: (0, 0, 0))
    const = lambda i: (0, 0)
    row = lambda i: (i, 0)
    weight = lambda a, b: pl.BlockSpec((None, a, b), lambda i: (layer, 0, 0), pipeline_mode=pl.Buffered(1))
    out_shapes = (jax.ShapeDtypeStruct((n, CONV_W), F32), jax.ShapeDtypeStruct((n, GDN_VAL_W), F32),
                  jax.ShapeDtypeStruct((n, LANES), F32), jax.ShapeDtypeStruct((n, ATT_W), BF16),
                  jax.ShapeDtypeStruct((n, ATT_W), BF16), jax.ShapeDtypeStruct((n, ATT_W), BF16))
    return pl.pallas_call(
        _in_kernel,
        grid=(n // tm,),
        in_specs=[pl.BlockSpec((tm, d), row),
                  pl.BlockSpec((1, 1, d), mod_map), pl.BlockSpec((1, 1, d), mod_map),
                  pl.BlockSpec((1, d), const), weight(d, IN_W),
                  pl.BlockSpec((1, Q_LORA), const), weight(Q_LORA, ATT_W),
                  pl.BlockSpec((1, KV_LORA), const), weight(KV_LORA, 2 * ATT_W),
                  pl.BlockSpec((1, ATT_W), const),
                  pl.BlockSpec((tm, 4 * LANES), lambda i: (i % per_b, 0))],
        out_specs=[pl.BlockSpec((tm, CONV_W), row), pl.BlockSpec((tm, GDN_VAL_W), row),
                   pl.BlockSpec((tm, LANES), row), pl.BlockSpec((tm, ATT_W), row),
                   pl.BlockSpec((tm, ATT_W), row), pl.BlockSpec((tm, ATT_W), row)],
        out_shape=out_shapes,
        compiler_params=pltpu.CompilerParams(dimension_semantics=("arbitrary",),
                                             vmem_limit_bytes=VMEM_LIMIT),
        name="in_proj",
    )(x, sh, sc, g, win, qg, wq, kvg, wkv, vone, tab)


def _conv_silu(x_ref, w_ref, pad_s):
    t = x_ref.shape[0]
    zero = jnp.zeros((SUBLANES, LANES), F32)
    pad_s[0:SUBLANES, :] = zero
    pad_s[SUBLANES + t:2 * SUBLANES + t, :] = zero
    pad_s[SUBLANES:SUBLANES + t, :] = x_ref[...]
    acc = None
    for j in range(CONV_K):
        off = SUBLANES - CONV_K // 2 + j
        term = pad_s[off:off + t, :] * w_ref[j:j + 1, :]
        acc = term if acc is None else acc + term
    return _silu(acc)


def _hi_lo(x):
    hi = x.astype(BF16).astype(F32)
    return hi, (x - hi).astype(BF16).astype(F32)


def _split3(x, axis):
    hi = x.astype(BF16).astype(F32)
    mid = (x - hi).astype(BF16).astype(F32)
    return jnp.concatenate([hi, mid, x - hi - mid], axis=axis).astype(BF16)


def _seg_sum(x, gmat2):
    hi, lo = _hi_lo(x)
    return _mm(jnp.concatenate([hi, lo], axis=1).astype(BF16), gmat2)


def _bd_rows(x, left):
    return jnp.concatenate([jnp.where(left, x, 0.0), jnp.where(left, 0.0, x)], axis=0)


def _bd_w(x, left):
    return _bd_rows(x, left).astype(BF16)


def _tri_inverse_pair(amats, lvl_mask, eye, left):
    c = amats[0].shape[0]
    n1 = [jnp.where(lvl_mask[0], -a, 0.0) for a in amats]
    n2 = [_mm(n.astype(BF16), _bd_w(n, left)) for n in n1]
    t = [eye + n for n in n1]
    y = [_mm(jnp.concatenate([x, n], axis=0).astype(BF16), _bd_w(n, left)) for x, n in zip(t, n2)]
    t = [x + yy[0:c] for x, yy in zip(t, y)]
    t = [x + _mm(x.astype(BF16), _bd_w(yy[c:2 * c], left)) for x, yy in zip(t, y)]
    for m in lvl_mask[1:]:
        tq = [_mm(x.astype(BF16), _bd_w(jnp.where(m, a, 0.0), left)) for x, a in zip(t, amats)]
        t = [x - _mm(y_.astype(BF16), _bd_w(x, left)) for x, y_ in zip(t, tq)]
    return t


def _gdn_kernel(qc_ref, kc_ref, vc_ref, mc_ref, zc_ref, ql_ref, kl_ref, vl_ref, ml_ref, zl_ref,
                wq_ref, wk_ref, wv_ref, alog_ref, dtb_ref, gn_ref,
                yc_ref, yl_ref,
                q_s, k_s, v_s, gate_s, gcc_s, bb_s, acc_s, pad_s):
    pair = pl.program_id(1)
    c = CHUNK
    tc = qc_ref.shape[0]
    tl = ql_ref.shape[0]
    tt = tc + tl
    ngate = 2 * GDN_HEADS
    ri = lax.broadcasted_iota(jnp.int32, (c, LANES), 0)
    li = lax.broadcasted_iota(jnp.int32, (c, LANES), 1)
    ci = li & (c - 1)
    left = li < c
    incl = (ri >= ci, ri <= ci)
    strict = (ri > ci, ri < ci)
    eye = (ri == ci).astype(F32)
    bi, bj = ri >> 3, ci >> 3
    lvl_mask = (bi == bj,
                ((bi >> 1) == (bj >> 1)) & (bi != bj),
                ((bi >> 2) == (bj >> 2)) & ((bi >> 1) != (bj >> 1)),
                (bi >> 2) != (bj >> 2))
    rr = lax.broadcasted_iota(jnp.int32, (LANES, LANES), 0)
    ll = lax.broadcasted_iota(jnp.int32, (LANES, LANES), 1)
    bd = (rr >> 6) == (ll >> 6)
    top = rr < c
    gmat = bd.astype(BF16)
    gmat2 = jnp.concatenate([gmat, gmat], axis=0)
    cum_p = (bd & (ll <= rr)).astype(BF16)
    cum_s = (bd & (ll >= rr)).astype(BF16)
    cum3 = (jnp.concatenate([cum_p] * 3, axis=1), jnp.concatenate([cum_s] * 3, axis=1))

    def stage(q_ref, k_ref, v_ref, m_ref, base, t):
        uq = _conv_silu(q_ref, wq_ref, pad_s)
        q_s[base:base + t, :] = uq * lax.rsqrt(_seg_sum(uq * uq, gmat2) + EPS) * (GDN_DK ** -0.5)
        uk = _conv_silu(k_ref, wk_ref, pad_s)
        k_s[base:base + t, :] = uk * lax.rsqrt(_seg_sum(uk * uk, gmat2) + EPS)
        v_s[base:base + t, :] = _conv_silu(v_ref, wv_ref, pad_s)
        m = m_ref[...]
        a = m[:, 0:ngate] + dtb_ref[...]
        sp = jnp.maximum(a, 0.0) + jnp.log(1.0 + jnp.exp(-jnp.abs(a)))
        gate_s[base:base + t, 0:ngate] = -jnp.exp(alog_ref[...]) * sp
        gate_s[base:base + t, ngate:2 * ngate] = _sigmoid(m[:, ngate:2 * ngate])

    gate_s[...] = jnp.zeros_like(gate_s)
    acc_s[...] = jnp.zeros_like(acc_s)
    stage(qc_ref, kc_ref, vc_ref, mc_ref, 0, tc)
    stage(ql_ref, kl_ref, vl_ref, ml_ref, tc, tl)

    src = lax.broadcasted_iota(jnp.int32, (3 * LANES, 4 * LANES), 0) & (LANES - 1)
    col = lax.broadcasted_iota(jnp.int32, (3 * LANES, 4 * LANES), 1)
    quantity, hh = col >> 7, (col >> 6) & 1
    want = (quantity & 1) * GDN_HEADS + (quantity >> 1) * ngate + 2 * pair + hh
    expand = (src == want).astype(BF16)
    out = _mm(_split3(gate_s[...], 1), expand)
    for d in range(2):
        gcc_s[d] = out[:, d * LANES:(d + 1) * LANES]
        bb_s[d] = out[:, (2 + d) * LANES:(3 + d) * LANES]

    def cumsum_tiles(i, carry):
        rows = pl.ds(pl.multiple_of(i * LANES, LANES), LANES)
        for d in range(2):
            gcc_s[d, rows, :] = _mm(cum3[d], _split3(gcc_s[d, rows, :], 0))
        return carry

    lax.fori_loop(0, tt // LANES, cumsum_tiles, 0, unroll=2)

    def chunk_step(streams, starts, states):
        n = range(len(streams))
        dirs = [d for d, u in streams]
        rows = [pl.ds(s, c) for s in starts]
        q = [q_s[r, :] for r in rows]
        k = [k_s[r, :] for r in rows]
        v = [v_s[r, :] for r in rows]
        gcc = [gcc_s[d, r, :] for d, r in zip(dirs, rows)]
        bb = [bb_s[d, r, :] for d, r in zip(dirs, rows)]
        grow = [jnp.sum(eye * g, axis=0, keepdims=True) for g in gcc]
        decay = [jnp.where(incl[d], jnp.exp(jnp.where(incl[d], gcc[i] - grow[i], 0.0)), 0.0)
                 for i, d in enumerate(dirs)]
        kb = [k[i] * bb[i] for i in n]
        kt = [jnp.transpose(k[i]) for i in n]
        kbd = [jnp.where(bd, jnp.concatenate([kt[i], kt[i]], axis=1), 0.0).astype(BF16) for i in n]
        pm = [_mm(jnp.concatenate([kb[i], q[i]], axis=0).astype(BF16), kbd[i]) for i in n]
        amat = [jnp.where(strict[d], pm[i][0:c] * decay[i], 0.0) for i, d in enumerate(dirs)]
        qk = [(pm[i][c:2 * c] * decay[i]).astype(BF16) for i in n]
        tinv = _tri_inverse_pair(amat, lvl_mask, eye, left)
        egc = [jnp.exp(g) for g in gcc]
        gl = [gcc[i][c - 1:c, :] if d == 0 else gcc[i][0:1, :] for i, d in enumerate(dirs)]
        rhs = [jnp.concatenate([_bd_rows(kb[i] * egc[i], left), _bd_rows(v[i] * bb[i], left)], axis=1).astype(BF16)
               for i in n]
        wu = [_mm(tinv[i].astype(BF16), rhs[i]) for i in n]
        wub = [x.astype(BF16) for x in wu]
        wbd = [jnp.concatenate([_bd_rows(wu[i][:, 0:LANES], left), _bd_rows(wu[i][:, LANES:], left)],
                               axis=1).astype(BF16) for i in n]
        kscale = [jnp.exp(gl[i] - grow[i]) for i in n]
        kgt = [(kt[i] * jnp.where(top[:, 0:c], kscale[i][:, 0:c], kscale[i][:, c:])).astype(BF16) for i in n]
        qkwu = [_mm(qk[i], wbd[i]) for i in n]
        kgwu = [_mm(kgt[i], wub[i]) for i in n]
        lhs = [jnp.concatenate([jnp.where(bd, -kgwu[i][:, 0:LANES], 0.0),
                                q[i] * egc[i] - qkwu[i][:, 0:LANES]], axis=0).astype(BF16) for i in n]
        nmat = [jnp.where(bd, kgwu[i][:, LANES:], 0.0) for i in n]
        egl = [jnp.exp(g) for g in gl]
        gsc = [jnp.where(top, egl[i][:, 0:1], egl[i][:, c:c + 1]) for i in n]
        st = list(states)
        for u in sorted({u for d, u in streams}):
            sel = [i for i in n if streams[i][1] == u]
            mqs = {i: _mm(lhs[i], st[dirs[i]].astype(BF16)) for i in sel}
            for i in sel:
                d = dirs[i]
                st[d] = st[d] * gsc[i] + mqs[i][0:LANES] + nmat[i]
                acc_s[rows[i], :] = acc_s[rows[i], :] + qkwu[i][:, LANES:] + mqs[i][LANES:]
        return tuple(st)

    def run(base, nchunks, nsub, states):
        streams = [(d, u) for u in range(nsub) for d in range(2)]

        def body(i, st):
            pos = [i * nsub + u for d, u in streams]
            idx = [p if d == 0 else nchunks - 1 - p for p, (d, u) in zip(pos, streams)]
            return chunk_step(streams, [pl.multiple_of(base + x * c, c) for x in idx], st)
        return lax.fori_loop(0, nchunks // nsub, body, states)

    zero = jnp.zeros((LANES, LANES), F32)
    states = run(0, tc // c, math.gcd(tc // c, GDN_SUB), (zero, zero))
    run(tc, tl // c, math.gcd(tl // c, GDN_SUB), states)

    def finish(z_ref, y_ref, base, t):
        o = acc_s[base:base + t, :]
        ms = _seg_sum(o * o, gmat2) * (1.0 / GDN_DV)
        y_ref[...] = (o * lax.rsqrt(ms + EPS) * gn_ref[...] * _silu(z_ref[...])).astype(y_ref.dtype)

    finish(zc_ref, yc_ref, 0, tc)
    finish(zl_ref, yl_ref, tc, tl)


def _gdn(qkv_c, misc_c, z_c, qkv_l, misc_l, z_l, conv_w, alog16, dtb16, gn2, batch, tc, tl):
    npair = GDN_HEADS // 2
    kq, kk, kvv = 0, npair, 2 * npair
    blk = lambda t, off: pl.BlockSpec((t, LANES), lambda b, p: (b, off + p))
    mblk = lambda t: pl.BlockSpec((t, LANES), lambda b, p: (b, 0))
    wblk = lambda off: pl.BlockSpec((CONV_K, LANES), lambda b, p: (0, off + p))
    const = lambda b, p: (0, 0)
    tt = tc + tl
    seq_buf = pltpu.VMEM((tt, LANES), F32)
    return pl.pallas_call(
        _gdn_kernel,
        grid=(batch, npair),
        in_specs=[blk(tc, kq), blk(tc, kk), blk(tc, kvv), mblk(tc), blk(tc, 0),
                  blk(tl, kq), blk(tl, kk), blk(tl, kvv), mblk(tl), blk(tl, 0),
                  wblk(kq), wblk(kk), wblk(kvv),
                  pl.BlockSpec((1, 2 * GDN_HEADS), const), pl.BlockSpec((1, 2 * GDN_HEADS), const),
                  pl.BlockSpec((1, LANES), const)],
        out_specs=[blk(tc, 0), blk(tl, 0)],
        out_shape=(jax.ShapeDtypeStruct((batch * tc, GDN_VAL_W), BF16),
                   jax.ShapeDtypeStruct((batch * tl, GDN_VAL_W), BF16)),
        scratch_shapes=[seq_buf, seq_buf, seq_buf, seq_buf,
                        pltpu.VMEM((2, tt, LANES), F32), pltpu.VMEM((2, tt, LANES), F32), seq_buf,
                        pltpu.VMEM((max(tc, tl) + 2 * SUBLANES, LANES), F32)],
        compiler_params=pltpu.CompilerParams(dimension_semantics=("arbitrary", "arbitrary"),
                                             vmem_limit_bytes=VMEM_LIMIT),
        name="gdn",
    )(qkv_c, qkv_c, qkv_c, misc_c, z_c, qkv_l, qkv_l, qkv_l, misc_l, z_l,
      conv_w, conv_w, conv_w, alog16, dtb16, gn2)


def _attn_kernel(*refs, n_src):
    q_ref = refs[0]
    k_refs = refs[1:1 + n_src]
    v_refs = refs[1 + n_src:1 + 2 * n_src]
    o_ref = refs[1 + 2 * n_src]
    heads = range(2)
    sls = [slice(j * HEAD_SLOT, (j + 1) * HEAD_SLOT) for j in heads]
    scores = [[lax.dot_general(q_ref[:, sl], kr[:, sl], (((1,), (1,)), ((), ())), preferred_element_type=F32)
               for kr in k_refs] for sl in sls]
    for j in heads:
        m = None
        for s in scores[j]:
            ms = jnp.max(s, axis=-1, keepdims=True)
            m = ms if m is None else jnp.maximum(m, ms)
        acc = None
        for s, vr in zip(scores[j], v_refs):
            p = jnp.exp(s - m).astype(BF16)
            t = _mm(p, vr[:, sls[j]])
            acc = t if acc is None else acc + t
        out = acc[:, 0:V_DIM] / acc[:, V_DIM:V_DIM + 1]
        o_ref[:, j * V_DIM:(j + 1) * V_DIM] = out.astype(o_ref.dtype)


def _attention(q, ks, vs, batch, tq_total, tq, src_lens):
    n_src = len(ks)
    nq = tq_total // tq
    npair = MLA_HEADS // 2
    qmap = lambda b, p, i: (b * nq + i, p)
    kvmap = lambda b, p, i: (b, p)
    in_specs = [pl.BlockSpec((tq, 2 * HEAD_SLOT), qmap)]
    in_specs += [pl.BlockSpec((t, 2 * HEAD_SLOT), kvmap) for t in src_lens]
    in_specs += [pl.BlockSpec((t, 2 * HEAD_SLOT), kvmap) for t in src_lens]
    return pl.pallas_call(
        functools.partial(_attn_kernel, n_src=n_src),
        grid=(batch, npair, nq),
        in_specs=in_specs,
        out_specs=pl.BlockSpec((tq, 2 * V_DIM), qmap),
        out_shape=jax.ShapeDtypeStruct((batch * tq_total, MLA_HEADS * V_DIM), BF16),
        compiler_params=pltpu.CompilerParams(dimension_semantics=("arbitrary", "arbitrary", "arbitrary"),
                                             vmem_limit_bytes=VMEM_LIMIT),
        name="attention",
    )(q, *ks, *vs)


def _out_ffn_kernel(x_ref, yg_ref, ym_ref, wo_ref, g1_ref, sh2_ref, sc2_ref, g2_ref, n2_ref,
                    wg_ref, wu_ref, wd_ref, fn_ref, o_ref, *, final_norm, th):
    half = GDN_VAL_W
    y = _mm(yg_ref[...], wo_ref[0:half, :]) + _mm(ym_ref[...], wo_ref[half:, :])
    x1 = x_ref[...] + g1_ref[0] * y
    h2 = (_rms(x1, n2_ref[...]) * (1.0 + sc2_ref[0]) + sh2_ref[0]).astype(BF16)
    acc = None
    for j in range(FFN_HIDDEN // th):
        hs = slice(j * th, (j + 1) * th)
        act = (_silu(_mm(h2, wg_ref[:, hs])) * _mm(h2, wu_ref[:, hs])).astype(BF16)
        part = _mm(act, wd_ref[hs, :])
        acc = part if acc is None else acc + part
    x2 = x1 + g2_ref[0] * acc
    if final_norm:
        x2 = _rms(x2, fn_ref[...])
    o_ref[...] = x2


def _out_ffn(x, yg, ym, wo, g1, sh2, sc2, g2, n2, wg, wu, wd, fn, seq, tm, th, final_norm, layer):
    n, d = x.shape
    per_b = seq // tm
    per_batch_mod = g1.shape[0] > 1
    mod_map = (lambda i: (i // per_b, 0, 0)) if per_batch_mod else (lambda i: (0, 0, 0))
    const = lambda i: (0, 0)
    row = lambda i: (i, 0)
    hid = FFN_HIDDEN
    mod_spec = pl.BlockSpec((1, 1, d), mod_map)
    resident = lambda shape: pl.BlockSpec((None,) + shape, lambda i: (layer, 0, 0),
                                          pipeline_mode=pl.Buffered(1))
    return pl.pallas_call(
        functools.partial(_out_ffn_kernel, final_norm=final_norm, th=th),
        grid=(n // tm,),
        in_specs=[pl.BlockSpec((tm, d), row),
                  pl.BlockSpec((tm, GDN_VAL_W), row), pl.BlockSpec((tm, MLA_HEADS * V_DIM), row),
                  resident((d, d)),
                  mod_spec, mod_spec, mod_spec, mod_spec,
                  pl.BlockSpec((1, d), const),
                  resident((d, hid)), resident((d, hid)), resident((hid, d)),
                  pl.BlockSpec((1, d), const)],
        out_specs=pl.BlockSpec((tm, d), row),
        out_shape=jax.ShapeDtypeStruct((n, d), F32),
        compiler_params=pltpu.CompilerParams(dimension_semantics=("arbitrary",),
                                             vmem_limit_bytes=VMEM_LIMIT),
        name="out_ffn",
    )(x, yg, ym, wo, g1, sh2, sc2, g2, n2, wg, wu, wd, fn)


_ROT_PERM = np.concatenate([np.arange(8, 16), np.arange(0, 8), np.arange(24, 32), np.arange(16, 24)])
_ROT_SIGN = np.concatenate([-np.ones(8), np.ones(8), -np.ones(8), np.ones(8)]).astype(np.float32)


def _rot_cols(w):
    return w[..., _ROT_PERM] * _ROT_SIGN


def _arrange_w_in(w_in):
    o = np.cumsum([0, CONV_W, GDN_VAL_W, 2 * GDN_HEADS, 2 * GDN_HEADS, Q_LORA, KV_LORA, QK_ROPE])
    qkv, z, a, b, cq, ckv, kr = (w_in[:, o[i]:o[i + 1]] for i in range(7))
    misc = jnp.concatenate([a, b, jnp.zeros((w_in.shape[0], LANES - 4 * GDN_HEADS - 2 * QK_ROPE), w_in.dtype),
                            kr, _rot_cols(kr)], axis=1)
    return jnp.concatenate([qkv, z, cq, ckv, misc], axis=1).astype(BF16)


def _arrange_w_uq(w_uq):
    w = w_uq.reshape(Q_LORA, MLA_HEADS, QK_NOPE + QK_ROPE)
    pe = w[:, :, QK_NOPE:]
    return jnp.concatenate([w[:, :, :QK_NOPE], pe, _rot_cols(pe)], axis=-1).reshape(Q_LORA, ATT_W).astype(BF16)


def _arrange_w_ukv(w_ukv):
    w = w_ukv.reshape(KV_LORA, MLA_HEADS, QK_NOPE + V_DIM)
    zk = jnp.zeros((KV_LORA, MLA_HEADS, HEAD_SLOT - QK_NOPE), w.dtype)
    zv = jnp.zeros((KV_LORA, MLA_HEADS, HEAD_SLOT - V_DIM), w.dtype)
    wk = jnp.concatenate([w[:, :, :QK_NOPE], zk], axis=-1).reshape(KV_LORA, ATT_W)
    wv = jnp.concatenate([w[:, :, QK_NOPE:], zv], axis=-1).reshape(KV_LORA, ATT_W)
    return jnp.concatenate([wk, wv], axis=1).astype(BF16)


def _rope_tables(t_lat, t_ctx):
    rows = t_lat // GRID_W
    row = jnp.repeat(jnp.arange(rows), GRID_W).astype(F32)
    col = jnp.tile(jnp.arange(GRID_W), rows).astype(F32)
    inv_freq = ROPE_THETA ** (-jnp.arange(0, AXIS_DIM, 2, dtype=F32) / AXIS_DIM)

    def axis_angles(pos):
        ang = pos[:, None] * inv_freq[None, :]
        return jnp.concatenate([ang, ang], axis=-1)

    ang = jnp.concatenate([axis_angles(row), axis_angles(col)], axis=-1)
    cos, sin = jnp.cos(ang), jnp.sin(ang)
    scale = (QK_NOPE + QK_ROPE) ** -0.5

    def build(cos_t, sin_t):
        t = cos_t.shape[0]
        z32 = jnp.zeros((t, QK_ROPE), F32)
        z64 = jnp.zeros((t, QK_NOPE), F32)
        one64 = jnp.ones((t, QK_NOPE), F32)
        cosq = jnp.concatenate([one64, cos_t, z32], axis=1) * scale
        sinq = jnp.concatenate([z64, sin_t, z32], axis=1) * scale
        cosk = jnp.concatenate([z64, cos_t, z32], axis=1)
        sink = jnp.concatenate([z64, sin_t, z32], axis=1)
        return jnp.concatenate([cosq, sinq, cosk, sink], axis=1)

    lat = build(cos, sin)
    ctx = build(jnp.ones((t_ctx, QK_ROPE), F32), jnp.zeros((t_ctx, QK_ROPE), F32))
    return lat, ctx


def kernel(x, c, ctx, c_ctx, w_mod, b_mod, norm1_g, norm2_g, w_in, conv_w, a_log, dt_bias, gdn_norm_g,
           q_norm_g, w_uq, kv_norm_g, w_ukv, w_out, w_gate, w_up, w_down, final_norm_g):
    batch, seq, d = x.shape
    tctx = ctx.shape[1]
    tab_l, tab_c = _rope_tables(seq, tctx)
    vone = jnp.zeros((MLA_HEADS, HEAD_SLOT), F32).at[:, V_DIM].set(1.0).reshape(1, ATT_W)
    cc = jnp.concatenate([c, c_ctx[None, :], jnp.zeros((16 - batch - 1, d), F32)], axis=0)
    xs = x.reshape(batch * seq, d)
    cs = ctx.reshape(batch * tctx, d)
    tm_l, tm_c, th = 256, 256, 1408

    mod_all = _modulation(cc, w_mod.astype(BF16), b_mod[:, None, :])
    win = jnp.stack([_arrange_w_in(w_in[l]) for l in range(DEPTH)])
    wq = jnp.stack([_arrange_w_uq(w_uq[l]) for l in range(DEPTH)])
    wkv = jnp.stack([_arrange_w_ukv(w_ukv[l]) for l in range(DEPTH)])
    wo, wg, wu, wd = w_out.astype(BF16), w_gate.astype(BF16), w_up.astype(BF16), w_down.astype(BF16)

    for l in range(DEPTH):
        last = l == DEPTH - 1
        parts = [mod_all[l, :, i * d:(i + 1) * d] for i in range(6)]
        lat = [p[:batch].reshape(batch, 1, d) for p in parts]
        cxt = [p[batch:batch + 1].reshape(1, 1, d) for p in parts]
        n1 = norm1_g[l][None, :]
        qg = q_norm_g[l][None, :]
        kvg = kv_norm_g[l][None, :]

        qkv_c, z_c, misc_c, q_c, k_c, v_c = _in_proj(cs, cxt[0], cxt[1], n1, win, qg, wq, kvg, wkv, vone,
                                                     tab_c, tctx, tm_c, l)
        qkv_l, z_l, misc_l, q_l, k_l, v_l = _in_proj(xs, lat[0], lat[1], n1, win, qg, wq, kvg, wkv, vone,
                                                     tab_l, seq, tm_l, l)
        yg_c, yg_l = _gdn(qkv_c, misc_c, z_c, qkv_l, misc_l, z_l, conv_w[l],
                          a_log[l].reshape(1, -1), dt_bias[l].reshape(1, -1),
                          jnp.tile(gdn_norm_g[l], 2)[None, :], batch, tctx, seq)
        ym_l = _attention(q_l, (k_c, k_l), (v_c, v_l), batch, seq, 512, (tctx, seq))

        n2 = norm2_g[l][None, :]
        fn = final_norm_g[None, :]
        xs = _out_ffn(xs, yg_l, ym_l, wo, lat[2], lat[3], lat[4], lat[5], n2, wg, wu, wd, fn,
                      seq, tm_l * 2, th, last, l)
        if not last:
            ym_c = _attention(q_c, (k_c,), (v_c,), batch, tctx, tctx, (tctx,))
            cs = _out_ffn(cs, yg_c, ym_c, wo, cxt[2], cxt[3], cxt[4], cxt[5], n2, wg, wu, wd, fn,
                          tctx, tm_c, th, False, l)
    return xs.reshape(batch, seq, d)
```

```python
import functools
import math

import jax
import jax.numpy as jnp
import numpy as np
from jax import lax
from jax.experimental import pallas as pl
from jax.experimental.pallas import tpu as pltpu

D_MODEL = 1024
DEPTH = 2
GRID_W = 64
EPS = 1e-6

GDN_HEADS = 8
GDN_DK = 64
GDN_DV = 64
GDN_KEY_W = GDN_HEADS * GDN_DK
GDN_VAL_W = GDN_HEADS * GDN_DV
CONV_W = 2 * GDN_KEY_W + GDN_VAL_W
CONV_K = 5
CHUNK = 64
GDN_SUB = 8

MLA_HEADS = 8
QK_NOPE = 64
QK_ROPE = 32
V_DIM = 64
Q_LORA = 256
KV_LORA = 128
ROPE_THETA = 10000.0
AXIS_DIM = QK_ROPE // 2
FFN_HIDDEN = -(-8 * D_MODEL // (3 * 256)) * 256

LANES = 128
SUBLANES = 8
HEAD_SLOT = LANES
ATT_W = MLA_HEADS * HEAD_SLOT
IN_W = CONV_W + GDN_VAL_W + Q_LORA + KV_LORA + LANES
VMEM_LIMIT = 56 * 1024 * 1024

F32 = jnp.float32
BF16 = jnp.bfloat16


def _sigmoid(x):
    return 0.5 * (1.0 + jnp.tanh(0.5 * x))


def _silu(x):
    return x * _sigmoid(x)


def _rms(x, g):
    return x * lax.rsqrt(jnp.mean(x * x, axis=-1, keepdims=True) + EPS) * g


def _mm(a, b):
    return jnp.dot(a, b, preferred_element_type=F32)


def _mod_kernel(c_ref, w_ref, b_ref, o_ref):
    o_ref[...] = _mm(_silu(c_ref[...]).astype(BF16), w_ref[...].astype(BF16)) + b_ref[...]


def _modulation(cc, w_mod, b_mod):
    rows, d = cc.shape
    depth, _, n = w_mod.shape
    tn = 1024
    return pl.pallas_call(
        _mod_kernel,
        grid=(depth, n // tn),
        in_specs=[pl.BlockSpec((rows, d), lambda l, j: (0, 0)),
                  pl.BlockSpec((None, d, tn), lambda l, j: (l, 0, j)),
                  pl.BlockSpec((None, 1, tn), lambda l, j: (l, 0, j))],
        out_specs=pl.BlockSpec((None, rows, tn), lambda l, j: (l, 0, j)),
        out_shape=jax.ShapeDtypeStruct((depth, rows, n), F32),
        compiler_params=pltpu.CompilerParams(dimension_semantics=("arbitrary", "arbitrary"),
                                             vmem_limit_bytes=VMEM_LIMIT),
        name="modulation",
    )(cc, w_mod, b_mod)


def _split3(x, axis):
    hi = x.astype(BF16).astype(F32)
    mid = (x - hi).astype(BF16).astype(F32)
    return jnp.concatenate([hi, mid, x - hi - mid], axis=axis).astype(BF16)


def _in_kernel(x_ref, sh_ref, sc_ref, g_ref, win_ref, qg_ref, wq_ref, kvg_ref, wkv_ref, vone_ref, tab_ref,
               alog_ref, dtb_ref, cump_ref, cums_ref,
               qkv_ref, z_ref, gate_ref, q_ref, k_ref, v_ref):
    x = x_ref[...]
    h = _rms(x, g_ref[...]) * (1.0 + sc_ref[0]) + sh_ref[0]
    p = _mm(h.astype(BF16), win_ref[...])
    o1 = CONV_W
    o2 = o1 + GDN_VAL_W
    o3 = o2 + Q_LORA
    o4 = o3 + KV_LORA
    qkv_ref[...] = p[:, :o1]
    z_ref[...] = p[:, o1:o2]
    misc = p[:, o4:]
    lane = lax.broadcasted_iota(jnp.int32, (1, LANES), 1)
    ngate = 2 * GDN_HEADS
    a = misc + dtb_ref[...]
    gdec = -jnp.exp(alog_ref[...]) * (jnp.maximum(a, 0.0) + jnp.log(1.0 + jnp.exp(-jnp.abs(a))))
    g3 = _split3(jnp.where(lane < ngate, gdec, 0.0), 0)
    gsum = jnp.where(lane < GDN_HEADS, _mm(cump_ref[...], g3), _mm(cums_ref[...], g3))
    gate_ref[...] = jnp.where(lane < ngate, gsum, jnp.where(lane < 2 * ngate, _sigmoid(misc), 0.0))
    cqn = _rms(p[:, o2:o3], qg_ref[...]).astype(BF16)
    ckvn = _rms(p[:, o3:o4], kvg_ref[...]).astype(BF16)
    qa = _mm(cqn, wq_ref[...])
    kv = _mm(ckvn, wkv_ref[...])
    tab = tab_ref[...]
    cosq, sinq = tab[:, 0:LANES], tab[:, LANES:2 * LANES]
    cosk, sink = tab[:, 2 * LANES:3 * LANES], tab[:, 3 * LANES:4 * LANES]
    kpe = misc * cosk + pltpu.roll(misc, LANES - QK_ROPE, 1) * sink
    for hd in range(MLA_HEADS):
        sl = slice(hd * HEAD_SLOT, (hd + 1) * HEAD_SLOT)
        qh = qa[:, sl]
        q_ref[:, sl] = (qh * cosq + pltpu.roll(qh, LANES - QK_ROPE, 1) * sinq).astype(BF16)
        k_ref[:, sl] = (kv[:, sl] + kpe).astype(BF16)
    v_ref[...] = (kv[:, ATT_W:] + vone_ref[...]).astype(BF16)


def _chunk_cumsum_mats(tm):
    i = np.arange(tm)
    same = (i[:, None] // CHUNK) == (i[None, :] // CHUNK)
    pre = (same & (i[None, :] <= i[:, None])).astype(np.float32)
    suf = (same & (i[None, :] >= i[:, None])).astype(np.float32)
    return jnp.asarray(np.tile(pre, (1, 3)), BF16), jnp.asarray(np.tile(suf, (1, 3)), BF16)


def _in_proj(x, sh, sc, g, win, qg, wq, kvg, wkv, vone, tab, alog, dtb, seq, tm, layer):
    n, d = x.shape
    cump, cums = _chunk_cumsum_mats(tm)
    per_b = seq // tm
    per_batch_mod = sh.shape[0] > 1
    mod_map = (lambda i: (i // per_b, 0, 0)) if per_batch_mod else (lambda i: (0, 0, 0))
    const = lambda i: (0, 0)
    row = lambda i: (i, 0)
    weight = lambda a, b: pl.BlockSpec((None, a, b), lambda i: (layer, 0, 0), pipeline_mode=pl.Buffered(1))
    out_shapes = (jax.ShapeDtypeStruct((n, CONV_W), F32), jax.ShapeDtypeStruct((n, GDN_VAL_W), F32),
                  jax.ShapeDtypeStruct((n, LANES), F32), jax.ShapeDtypeStruct((n, ATT_W), BF16),
                  jax.ShapeDtypeStruct((n, ATT_W), BF16), jax.ShapeDtypeStruct((n, ATT_W), BF16))
    return pl.pallas_call(
        _in_kernel,
        grid=(n // tm,),
        in_specs=[pl.BlockSpec((tm, d), row),
                  pl.BlockSpec((1, 1, d), mod_map), pl.BlockSpec((1, 1, d), mod_map),
                  pl.BlockSpec((1, d), const), weight(d, IN_W),
                  pl.BlockSpec((1, Q_LORA), const), weight(Q_LORA, ATT_W),
                  pl.BlockSpec((1, KV_LORA), const), weight(KV_LORA, 2 * ATT_W),
                  pl.BlockSpec((1, ATT_W), const),
                  pl.BlockSpec((tm, 4 * LANES), lambda i: (i % per_b, 0)),
                  pl.BlockSpec((1, LANES), const), pl.BlockSpec((1, LANES), const),
                  pl.BlockSpec((tm, 3 * tm), const), pl.BlockSpec((tm, 3 * tm), const)],
        out_specs=[pl.BlockSpec((tm, CONV_W), row), pl.BlockSpec((tm, GDN_VAL_W), row),
                   pl.BlockSpec((tm, LANES), row), pl.BlockSpec((tm, ATT_W), row),
                   pl.BlockSpec((tm, ATT_W), row), pl.BlockSpec((tm, ATT_W), row)],
        out_shape=out_shapes,
        compiler_params=pltpu.CompilerParams(dimension_semantics=("arbitrary",),
                                             vmem_limit_bytes=VMEM_LIMIT),
        name="in_proj",
    )(x, sh, sc, g, win, qg, wq, kvg, wkv, vone, tab, alog, dtb, cump, cums)


def _conv_silu(x_ref, w_ref, pad_s):
    t = x_ref.shape[0]
    zero = jnp.zeros((SUBLANES, LANES), F32)
    pad_s[0:SUBLANES, :] = zero
    pad_s[SUBLANES + t:2 * SUBLANES + t, :] = zero
    pad_s[SUBLANES:SUBLANES + t, :] = x_ref[...]
    acc = None
    for j in range(CONV_K):
        off = SUBLANES - CONV_K // 2 + j
        term = pad_s[off:off + t, :] * w_ref[j:j + 1, :]
        acc = term if acc is None else acc + term
    return _silu(acc)


def _hi_lo(x):
    hi = x.astype(BF16).astype(F32)
    return hi, (x - hi).astype(BF16).astype(F32)


def _seg_sum(x, gmat2):
    hi, lo = _hi_lo(x)
    return _mm(jnp.concatenate([hi, lo], axis=1).astype(BF16), gmat2)


def _bd_rows(x, left):
    return jnp.concatenate([jnp.where(left, x, 0.0), jnp.where(left, 0.0, x)], axis=0)


def _bd_w(x, left):
    return _bd_rows(x, left).astype(BF16)


def _tri_inverse_pair(amats, lvl_mask, eye, left):
    c = amats[0].shape[0]
    n1 = [jnp.where(lvl_mask[0], -a, 0.0) for a in amats]
    n2 = [_mm(n.astype(BF16), _bd_w(n, left)) for n in n1]
    t = [eye + n for n in n1]
    y = [_mm(jnp.concatenate([x, n], axis=0).astype(BF16), _bd_w(n, left)) for x, n in zip(t, n2)]
    t = [x + yy[0:c] for x, yy in zip(t, y)]
    t = [x + _mm(x.astype(BF16), _bd_w(yy[c:2 * c], left)) for x, yy in zip(t, y)]
    for m in lvl_mask[1:]:
        tq = [_mm(x.astype(BF16), _bd_w(jnp.where(m, a, 0.0), left)) for x, a in zip(t, amats)]
        t = [x - _mm(y_.astype(BF16), _bd_w(x, left)) for x, y_ in zip(t, tq)]
    return t


def _gdn_kernel(qc_ref, kc_ref, vc_ref, gc_ref, zc_ref, ql_ref, kl_ref, vl_ref, gl_ref, zl_ref,
                wq_ref, wk_ref, wv_ref, gn_ref,
                yc_ref, yl_ref,
                q_s, k_s, v_s, gcc_s, bb_s, acc_s, pad_s):
    pair = pl.program_id(1)
    c = CHUNK
    tc = qc_ref.shape[0]
    tl = ql_ref.shape[0]
    ngate = 2 * GDN_HEADS
    ri = lax.broadcasted_iota(jnp.int32, (c, LANES), 0)
    li = lax.broadcasted_iota(jnp.int32, (c, LANES), 1)
    ci = li & (c - 1)
    left = li < c
    incl = (ri >= ci, ri <= ci)
    strict = (ri > ci, ri < ci)
    eye = (ri == ci).astype(F32)
    bi, bj = ri >> 3, ci >> 3
    lvl_mask = (bi == bj,
                ((bi >> 1) == (bj >> 1)) & (bi != bj),
                ((bi >> 2) == (bj >> 2)) & ((bi >> 1) != (bj >> 1)),
                (bi >> 2) != (bj >> 2))
    rr = lax.broadcasted_iota(jnp.int32, (LANES, LANES), 0)
    ll = lax.broadcasted_iota(jnp.int32, (LANES, LANES), 1)
    bd = (rr >> 6) == (ll >> 6)
    top = rr < c
    gmat = bd.astype(BF16)
    gmat2 = jnp.concatenate([gmat, gmat], axis=0)
    src = lax.broadcasted_iota(jnp.int32, (3 * LANES, 4 * LANES), 0) & (LANES - 1)
    col = lax.broadcasted_iota(jnp.int32, (3 * LANES, 4 * LANES), 1)
    quantity, hh = col >> 7, (col >> 6) & 1
    want = (quantity & 1) * GDN_HEADS + (quantity >> 1) * ngate + 2 * pair + hh
    expand = (src == want).astype(BF16)

    def stage(q_ref, k_ref, v_ref, g_ref, base, t):
        uq = _conv_silu(q_ref, wq_ref, pad_s)
        q_s[base:base + t, :] = uq * lax.rsqrt(_seg_sum(uq * uq, gmat2) + EPS) * (GDN_DK ** -0.5)
        uk = _conv_silu(k_ref, wk_ref, pad_s)
        k_s[base:base + t, :] = uk * lax.rsqrt(_seg_sum(uk * uk, gmat2) + EPS)
        v_s[base:base + t, :] = _conv_silu(v_ref, wv_ref, pad_s)
        out = _mm(_split3(g_ref[...], 1), expand)
        for d in range(2):
            gcc_s[d, base:base + t, :] = out[:, d * LANES:(d + 1) * LANES]
            bb_s[d, base:base + t, :] = out[:, (2 + d) * LANES:(3 + d) * LANES]

    acc_s[...] = jnp.zeros_like(acc_s)
    stage(qc_ref, kc_ref, vc_ref, gc_ref, 0, tc)
    stage(ql_ref, kl_ref, vl_ref, gl_ref, tc, tl)

    def chunk_step(streams, starts, states):
        n = range(len(streams))
        dirs = [d for d, u in streams]
        rows = [pl.ds(s, c) for s in starts]
        q = [q_s[r, :] for r in rows]
        k = [k_s[r, :] for r in rows]
        v = [v_s[r, :] for r in rows]
        gcc = [gcc_s[d, r, :] for d, r in zip(dirs, rows)]
        bb = [bb_s[d, r, :] for d, r in zip(dirs, rows)]
        grow = [jnp.sum(eye * g, axis=0, keepdims=True) for g in gcc]
        decay = [jnp.where(incl[d], jnp.exp(jnp.where(incl[d], gcc[i] - grow[i], 0.0)), 0.0)
                 for i, d in enumerate(dirs)]
        kb = [k[i] * bb[i] for i in n]
        kt = [jnp.transpose(k[i]) for i in n]
        kbd = [jnp.where(bd, jnp.concatenate([kt[i], kt[i]], axis=1), 0.0).astype(BF16) for i in n]
        pm = [_mm(jnp.concatenate([kb[i], q[i]], axis=0).astype(BF16), kbd[i]) for i in n]
        amat = [jnp.where(strict[d], pm[i][0:c] * decay[i], 0.0) for i, d in enumerate(dirs)]
        qk = [(pm[i][c:2 * c] * decay[i]).astype(BF16) for i in n]
        tinv = _tri_inverse_pair(amat, lvl_mask, eye, left)
        egc = [jnp.exp(g) for g in gcc]
        gl = [gcc[i][c - 1:c, :] if d == 0 else gcc[i][0:1, :] for i, d in enumerate(dirs)]
        rhs = [jnp.concatenate([_bd_rows(kb[i] * egc[i], left), _bd_rows(v[i] * bb[i], left)], axis=1).astype(BF16)
               for i in n]
        wu = [_mm(tinv[i].astype(BF16), rhs[i]) for i in n]
        wub = [x.astype(BF16) for x in wu]
        wbd = [jnp.concatenate([_bd_rows(wu[i][:, 0:LANES], left), _bd_rows(wu[i][:, LANES:], left)],
                               axis=1).astype(BF16) for i in n]
        kscale = [jnp.exp(gl[i] - grow[i]) for i in n]
        kgt = [(kt[i] * jnp.where(top[:, 0:c], kscale[i][:, 0:c], kscale[i][:, c:])).astype(BF16) for i in n]
        qkwu = [_mm(qk[i], wbd[i]) for i in n]
        kgwu = [_mm(kgt[i], wub[i]) for i in n]
        lhs = [jnp.concatenate([jnp.where(bd, -kgwu[i][:, 0:LANES], 0.0),
                                q[i] * egc[i] - qkwu[i][:, 0:LANES]], axis=0).astype(BF16) for i in n]
        nmat = [jnp.where(bd, kgwu[i][:, LANES:], 0.0) for i in n]
        egl = [jnp.exp(g) for g in gl]
        st = list(states)
        for u in sorted({u for d, u in streams}):
            sel = [i for i in n if streams[i][1] == u]
            mqs = {i: _mm(lhs[i], st[dirs[i]].astype(BF16)) for i in sel}
            for i in sel:
                d = dirs[i]
                st[d] = st[d] * egl[i] + mqs[i][0:LANES] + nmat[i]
                acc_s[rows[i], :] = acc_s[rows[i], :] + qkwu[i][:, LANES:] + mqs[i][LANES:]
        return tuple(st)

    def run(base, nchunks, nsub, states):
        streams = [(d, u) for u in range(nsub) for d in range(2)]

        def body(i, st):
            pos = [i * nsub + u for d, u in streams]
            idx = [p if d == 0 else nchunks - 1 - p for p, (d, u) in zip(pos, streams)]
            return chunk_step(streams, [pl.multiple_of(base + x * c, c) for x in idx], st)
        return lax.fori_loop(0, nchunks // nsub, body, states)

    zero = jnp.zeros((LANES, LANES), F32)
    states = run(0, tc // c, math.gcd(tc // c, GDN_SUB), (zero, zero))
    run(tc, tl // c, math.gcd(tl // c, GDN_SUB), states)

    def finish(z_ref, y_ref, base, t):
        o = acc_s[base:base + t, :]
        ms = _seg_sum(o * o, gmat2) * (1.0 / GDN_DV)
        y_ref[...] = (o * lax.rsqrt(ms + EPS) * gn_ref[...] * _silu(z_ref[...])).astype(y_ref.dtype)

    finish(zc_ref, yc_ref, 0, tc)
    finish(zl_ref, yl_ref, tc, tl)


def _gdn(qkv_c, gate_c, z_c, qkv_l, gate_l, z_l, conv_w, gn2, batch, tc, tl):
    npair = GDN_HEADS // 2
    kq, kk, kvv = 0, npair, 2 * npair
    blk = lambda t, off: pl.BlockSpec((t, LANES), lambda b, p: (b, off + p))
    mblk = lambda t: pl.BlockSpec((t, LANES), lambda b, p: (b, 0))
    wblk = lambda off: pl.BlockSpec((CONV_K, LANES), lambda b, p: (0, off + p))
    const = lambda b, p: (0, 0)
    tt = tc + tl
    seq_buf = pltpu.VMEM((tt, LANES), F32)
    return pl.pallas_call(
        _gdn_kernel,
        grid=(batch, npair),
        in_specs=[blk(tc, kq), blk(tc, kk), blk(tc, kvv), mblk(tc), blk(tc, 0),
                  blk(tl, kq), blk(tl, kk), blk(tl, kvv), mblk(tl), blk(tl, 0),
                  wblk(kq), wblk(kk), wblk(kvv),
                  pl.BlockSpec((1, LANES), const)],
        out_specs=[blk(tc, 0), blk(tl, 0)],
        out_shape=(jax.ShapeDtypeStruct((batch * tc, GDN_VAL_W), BF16),
                   jax.ShapeDtypeStruct((batch * tl, GDN_VAL_W), BF16)),
        scratch_shapes=[seq_buf, seq_buf, seq_buf,
                        pltpu.VMEM((2, tt, LANES), F32), pltpu.VMEM((2, tt, LANES), F32), seq_buf,
                        pltpu.VMEM((max(tc, tl) + 2 * SUBLANES, LANES), F32)],
        compiler_params=pltpu.CompilerParams(dimension_semantics=("arbitrary", "arbitrary"),
                                             vmem_limit_bytes=VMEM_LIMIT),
        name="gdn",
    )(qkv_c, qkv_c, qkv_c, gate_c, z_c, qkv_l, qkv_l, qkv_l, gate_l, z_l,
      conv_w, conv_w, conv_w, gn2)


def _attn_kernel(*refs, n_src):
    q_ref = refs[0]
    k_refs = refs[1:1 + n_src]
    v_refs = refs[1 + n_src:1 + 2 * n_src]
    o_ref = refs[1 + 2 * n_src]
    heads = range(2)
    sls = [slice(j * HEAD_SLOT, (j + 1) * HEAD_SLOT) for j in heads]
    scores = [[lax.dot_general(q_ref[:, sl], kr[:, sl], (((1,), (1,)), ((), ())), preferred_element_type=F32)
               for kr in k_refs] for sl in sls]
    for j in heads:
        m = None
        for s in scores[j]:
            ms = jnp.max(s, axis=-1, keepdims=True)
            m = ms if m is None else jnp.maximum(m, ms)
        acc = None
        for s, vr in zip(scores[j], v_refs):
            p = jnp.exp(s - m).astype(BF16)
            t = _mm(p, vr[:, sls[j]])
            acc = t if acc is None else acc + t
        out = acc[:, 0:V_DIM] / acc[:, V_DIM:V_DIM + 1]
        o_ref[:, j * V_DIM:(j + 1) * V_DIM] = out.astype(o_ref.dtype)


def _attention(q, ks, vs, batch, tq_total, tq, src_lens):
    n_src = len(ks)
    nq = tq_total // tq
    npair = MLA_HEADS // 2
    qmap = lambda b, p, i: (b * nq + i, p)
    kvmap = lambda b, p, i: (b, p)
    in_specs = [pl.BlockSpec((tq, 2 * HEAD_SLOT), qmap)]
    in_specs += [pl.BlockSpec((t, 2 * HEAD_SLOT), kvmap) for t in src_lens]
    in_specs += [pl.BlockSpec((t, 2 * HEAD_SLOT), kvmap) for t in src_lens]
    return pl.pallas_call(
        functools.partial(_attn_kernel, n_src=n_src),
        grid=(batch, npair, nq),
        in_specs=in_specs,
        out_specs=pl.BlockSpec((tq, 2 * V_DIM), qmap),
        out_shape=jax.ShapeDtypeStruct((batch * tq_total, MLA_HEADS * V_DIM), BF16),
        compiler_params=pltpu.CompilerParams(dimension_semantics=("arbitrary", "arbitrary", "arbitrary"),
                                             vmem_limit_bytes=VMEM_LIMIT),
        name="attention",
    )(q, *ks, *vs)


def _out_ffn_kernel(x_ref, yg_ref, ym_ref, wo_ref, g1_ref, sh2_ref, sc2_ref, g2_ref, n2_ref,
                    wg_ref, wu_ref, wd_ref, fn_ref, o_ref, *, final_norm, th):
    half = GDN_VAL_W
    y = _mm(yg_ref[...], wo_ref[0:half, :]) + _mm(ym_ref[...], wo_ref[half:, :])
    x1 = x_ref[...] + g1_ref[0] * y
    h2 = (_rms(x1, n2_ref[...]) * (1.0 + sc2_ref[0]) + sh2_ref[0]).astype(BF16)
    acc = None
    for j in range(FFN_HIDDEN // th):
        hs = slice(j * th, (j + 1) * th)
        act = (_silu(_mm(h2, wg_ref[:, hs])) * _mm(h2, wu_ref[:, hs])).astype(BF16)
        part = _mm(act, wd_ref[hs, :])
        acc = part if acc is None else acc + part
    x2 = x1 + g2_ref[0] * acc
    if final_norm:
        x2 = _rms(x2, fn_ref[...])
    o_ref[...] = x2


def _out_ffn(x, yg, ym, wo, g1, sh2, sc2, g2, n2, wg, wu, wd, fn, seq, tm, th, final_norm, layer):
    n, d = x.shape
    per_b = seq // tm
    per_batch_mod = g1.shape[0] > 1
    mod_map = (lambda i: (i // per_b, 0, 0)) if per_batch_mod else (lambda i: (0, 0, 0))
    const = lambda i: (0, 0)
    row = lambda i: (i, 0)
    hid = FFN_HIDDEN
    mod_spec = pl.BlockSpec((1, 1, d), mod_map)
    resident = lambda shape: pl.BlockSpec((None,) + shape, lambda i: (layer, 0, 0),
                                          pipeline_mode=pl.Buffered(1))
    return pl.pallas_call(
        functools.partial(_out_ffn_kernel, final_norm=final_norm, th=th),
        grid=(n // tm,),
        in_specs=[pl.BlockSpec((tm, d), row),
                  pl.BlockSpec((tm, GDN_VAL_W), row), pl.BlockSpec((tm, MLA_HEADS * V_DIM), row),
                  resident((d, d)),
                  mod_spec, mod_spec, mod_spec, mod_spec,
                  pl.BlockSpec((1, d), const),
                  resident((d, hid)), resident((d, hid)), resident((hid, d)),
                  pl.BlockSpec((1, d), const)],
        out_specs=pl.BlockSpec((tm, d), row),
        out_shape=jax.ShapeDtypeStruct((n, d), F32),
        compiler_params=pltpu.CompilerParams(dimension_semantics=("arbitrary",),
                                             vmem_limit_bytes=VMEM_LIMIT),
        name="out_ffn",
    )(x, yg, ym, wo, g1, sh2, sc2, g2, n2, wg, wu, wd, fn)


_ROT_PERM = np.concatenate([np.arange(8, 16), np.arange(0, 8), np.arange(24, 32), np.arange(16, 24)])
_ROT_SIGN = np.concatenate([-np.ones(8), np.ones(8), -np.ones(8), np.ones(8)]).astype(np.float32)


def _rot_cols(w):
    return w[..., _ROT_PERM] * _ROT_SIGN


def _arrange_w_in(w_in):
    o = np.cumsum([0, CONV_W, GDN_VAL_W, 2 * GDN_HEADS, 2 * GDN_HEADS, Q_LORA, KV_LORA, QK_ROPE])
    qkv, z, a, b, cq, ckv, kr = (w_in[:, o[i]:o[i + 1]] for i in range(7))
    misc = jnp.concatenate([a, b, jnp.zeros((w_in.shape[0], LANES - 4 * GDN_HEADS - 2 * QK_ROPE), w_in.dtype),
                            kr, _rot_cols(kr)], axis=1)
    return jnp.concatenate([qkv, z, cq, ckv, misc], axis=1).astype(BF16)


def _arrange_w_uq(w_uq):
    w = w_uq.reshape(Q_LORA, MLA_HEADS, QK_NOPE + QK_ROPE)
    pe = w[:, :, QK_NOPE:]
    return jnp.concatenate([w[:, :, :QK_NOPE], pe, _rot_cols(pe)], axis=-1).reshape(Q_LORA, ATT_W).astype(BF16)


def _arrange_w_ukv(w_ukv):
    w = w_ukv.reshape(KV_LORA, MLA_HEADS, QK_NOPE + V_DIM)
    zk = jnp.zeros((KV_LORA, MLA_HEADS, HEAD_SLOT - QK_NOPE), w.dtype)
    zv = jnp.zeros((KV_LORA, MLA_HEADS, HEAD_SLOT - V_DIM), w.dtype)
    wk = jnp.concatenate([w[:, :, :QK_NOPE], zk], axis=-1).reshape(KV_LORA, ATT_W)
    wv = jnp.concatenate([w[:, :, QK_NOPE:], zv], axis=-1).reshape(KV_LORA, ATT_W)
    return jnp.concatenate([wk, wv], axis=1).astype(BF16)


def _rope_tables(t_lat, t_ctx):
    rows = t_lat // GRID_W
    row = jnp.repeat(jnp.arange(rows), GRID_W).astype(F32)
    col = jnp.tile(jnp.arange(GRID_W), rows).astype(F32)
    inv_freq = ROPE_THETA ** (-jnp.arange(0, AXIS_DIM, 2, dtype=F32) / AXIS_DIM)

    def axis_angles(pos):
        ang = pos[:, None] * inv_freq[None, :]
        return jnp.concatenate([ang, ang], axis=-1)

    ang = jnp.concatenate([axis_angles(row), axis_angles(col)], axis=-1)
    cos, sin = jnp.cos(ang), jnp.sin(ang)
    scale = (QK_NOPE + QK_ROPE) ** -0.5

    def build(cos_t, sin_t):
        t = cos_t.shape[0]
        z32 = jnp.zeros((t, QK_ROPE), F32)
        z64 = jnp.zeros((t, QK_NOPE), F32)
        one64 = jnp.ones((t, QK_NOPE), F32)
        cosq = jnp.concatenate([one64, cos_t, z32], axis=1) * scale
        sinq = jnp.concatenate([z64, sin_t, z32], axis=1) * scale
        cosk = jnp.concatenate([z64, cos_t, z32], axis=1)
        sink = jnp.concatenate([z64, sin_t, z32], axis=1)
        return jnp.concatenate([cosq, sinq, cosk, sink], axis=1)

    lat = build(cos, sin)
    ctx = build(jnp.ones((t_ctx, QK_ROPE), F32), jnp.zeros((t_ctx, QK_ROPE), F32))
    return lat, ctx


def kernel(x, c, ctx, c_ctx, w_mod, b_mod, norm1_g, norm2_g, w_in, conv_w, a_log, dt_bias, gdn_norm_g,
           q_norm_g, w_uq, kv_norm_g, w_ukv, w_out, w_gate, w_up, w_down, final_norm_g):
    batch, seq, d = x.shape
    tctx = ctx.shape[1]
    tab_l, tab_c = _rope_tables(seq, tctx)
    vone = jnp.zeros((MLA_HEADS, HEAD_SLOT), F32).at[:, V_DIM].set(1.0).reshape(1, ATT_W)
    cc = jnp.concatenate([c, c_ctx[None, :], jnp.zeros((16 - batch - 1, d), F32)], axis=0)
    xs = x.reshape(batch * seq, d)
    cs = ctx.reshape(batch * tctx, d)
    tm_l, tm_c, th = 256, 256, 1408

    mod_all = _modulation(cc, w_mod, b_mod[:, None, :])
    win = jnp.stack([_arrange_w_in(w_in[l]) for l in range(DEPTH)])
    wq = jnp.stack([_arrange_w_uq(w_uq[l]) for l in range(DEPTH)])
    wkv = jnp.stack([_arrange_w_ukv(w_ukv[l]) for l in range(DEPTH)])
    wo, wg, wu, wd = w_out.astype(BF16), w_gate.astype(BF16), w_up.astype(BF16), w_down.astype(BF16)

    for l in range(DEPTH):
        last = l == DEPTH - 1
        parts = [mod_all[l, :, i * d:(i + 1) * d] for i in range(6)]
        lat = [p[:batch].reshape(batch, 1, d) for p in parts]
        cxt = [p[batch:batch + 1].reshape(1, 1, d) for p in parts]
        n1 = norm1_g[l][None, :]
        qg = q_norm_g[l][None, :]
        kvg = kv_norm_g[l][None, :]

        gate_pad = ((0, 0), (0, LANES - 2 * GDN_HEADS))
        alog = jnp.pad(a_log[l].reshape(1, -1), gate_pad)
        dtb = jnp.pad(dt_bias[l].reshape(1, -1), gate_pad)
        qkv_c, z_c, gate_c, q_c, k_c, v_c = _in_proj(cs, cxt[0], cxt[1], n1, win, qg, wq, kvg, wkv, vone,
                                                     tab_c, alog, dtb, tctx, tm_c, l)
        qkv_l, z_l, gate_l, q_l, k_l, v_l = _in_proj(xs, lat[0], lat[1], n1, win, qg, wq, kvg, wkv, vone,
                                                     tab_l, alog, dtb, seq, tm_l, l)
        yg_c, yg_l = _gdn(qkv_c, gate_c, z_c, qkv_l, gate_l, z_l, conv_w[l],
                          jnp.tile(gdn_norm_g[l], 2)[None, :], batch, tctx, seq)
        ym_l = _attention(q_l, (k_c, k_l), (v_c, v_l), batch, seq, min(seq, 1024), (tctx, seq))

        n2 = norm2_g[l][None, :]
        fn = final_norm_g[None, :]
        xs = _out_ffn(xs, yg_l, ym_l, wo, lat[2], lat[3], lat[4], lat[5], n2, wg, wu, wd, fn,
                      seq, tm_l * 2, th, last, l)
        if not last:
            ym_c = _attention(q_c, (k_c,), (v_c,), batch, tctx, tctx, (tctx,))
            cs = _out_ffn(cs, yg_c, ym_c, wo, cxt[2], cxt[3], cxt[4], cxt[5], n2, wg, wu, wd, fn,
                          tctx, tm_c, th, False, l)
    return xs.reshape(batch, seq, d)
```

```python
import functools
import math

import jax
import jax.numpy as jnp
import numpy as np
from jax import lax
from jax.experimental import pallas as pl
from jax.experimental.pallas import tpu as pltpu

D_MODEL = 1024
DEPTH = 2
GRID_W = 64
EPS = 1e-6

GDN_HEADS = 8
GDN_DK = 64
GDN_DV = 64
GDN_KEY_W = GDN_HEADS * GDN_DK
GDN_VAL_W = GDN_HEADS * GDN_DV
CONV_W = 2 * GDN_KEY_W + GDN_VAL_W
CONV_K = 5
CHUNK = 64
GDN_SUB = 8

MLA_HEADS = 8
QK_NOPE = 64
QK_ROPE = 32
V_DIM = 64
Q_LORA = 256
KV_LORA = 128
ROPE_THETA = 10000.0
AXIS_DIM = QK_ROPE // 2
FFN_HIDDEN = -(-8 * D_MODEL // (3 * 256)) * 256

LANES = 128
SUBLANES = 8
MXU_WIDTH = 256
HEAD_SLOT = LANES
ATT_W = MLA_HEADS * HEAD_SLOT
IN_W = CONV_W + GDN_VAL_W + Q_LORA + KV_LORA + LANES
VMEM_LIMIT = 56 * 1024 * 1024

F32 = jnp.float32
BF16 = jnp.bfloat16


def _sigmoid(x):
    return 0.5 * (1.0 + jnp.tanh(0.5 * x))


def _silu(x):
    return x * _sigmoid(x)


def _rms(x, g):
    return x * lax.rsqrt(jnp.mean(x * x, axis=-1, keepdims=True) + EPS) * g


def _mm(a, b):
    return jnp.dot(a, b, preferred_element_type=F32)


def _mod_kernel(c_ref, w_ref, b_ref, o_ref):
    o_ref[...] = _mm(_silu(c_ref[...]).astype(BF16), w_ref[...].astype(BF16)) + b_ref[...]


def _modulation(cc, w_mod, b_mod):
    rows, d = cc.shape
    depth, _, n = w_mod.shape
    tn = 1024
    return pl.pallas_call(
        _mod_kernel,
        grid=(depth, n // tn),
        in_specs=[pl.BlockSpec((rows, d), lambda l, j: (0, 0)),
                  pl.BlockSpec((None, d, tn), lambda l, j: (l, 0, j)),
                  pl.BlockSpec((None, 1, tn), lambda l, j: (l, 0, j))],
        out_specs=pl.BlockSpec((None, rows, tn), lambda l, j: (l, 0, j)),
        out_shape=jax.ShapeDtypeStruct((depth, rows, n), F32),
        compiler_params=pltpu.CompilerParams(dimension_semantics=("arbitrary", "arbitrary"),
                                             vmem_limit_bytes=VMEM_LIMIT),
        name="modulation",
    )(cc, w_mod, b_mod)


def _split3(x, axis):
    hi = x.astype(BF16).astype(F32)
    mid = (x - hi).astype(BF16).astype(F32)
    return jnp.concatenate([hi, mid, x - hi - mid], axis=axis).astype(BF16)


def _in_kernel(x_ref, sh_ref, sc_ref, g_ref, win_ref, qg_ref, wq_ref, kvg_ref, wkv_ref, vone_ref, tab_ref,
               alog_ref, dtb_ref, cump_ref, cums_ref,
               qkv_ref, z_ref, gate_ref, q_ref, k_ref, v_ref):
    x = x_ref[...]
    h = _rms(x, g_ref[...]) * (1.0 + sc_ref[...]) + sh_ref[...]
    p = _mm(h.astype(BF16), win_ref[...])
    o1 = CONV_W
    o2 = o1 + GDN_VAL_W
    o3 = o2 + Q_LORA
    o4 = o3 + KV_LORA
    qkv_ref[...] = p[:, :o1]
    z_ref[...] = p[:, o1:o2]
    misc = p[:, o4:]
    lane = lax.broadcasted_iota(jnp.int32, (1, LANES), 1)
    ngate = 2 * GDN_HEADS
    a = misc + dtb_ref[...]
    gdec = -jnp.exp(alog_ref[...]) * (jnp.maximum(a, 0.0) + jnp.log(1.0 + jnp.exp(-jnp.abs(a))))
    g3 = _split3(jnp.where(lane < ngate, gdec, 0.0), 0)
    gsum = jnp.where(lane < GDN_HEADS, _mm(cump_ref[...], g3), _mm(cums_ref[...], g3))
    gate_ref[...] = jnp.where(lane < ngate, gsum, jnp.where(lane < 2 * ngate, _sigmoid(misc), 0.0))
    cqn = _rms(p[:, o2:o3], qg_ref[...]).astype(BF16)
    ckvn = _rms(p[:, o3:o4], kvg_ref[...]).astype(BF16)
    qa = _mm(cqn, wq_ref[...])
    kv = _mm(ckvn, wkv_ref[...])
    tab = tab_ref[...]
    cosq, sinq = tab[:, 0:LANES], tab[:, LANES:2 * LANES]
    cosk, sink = tab[:, 2 * LANES:3 * LANES], tab[:, 3 * LANES:4 * LANES]
    kpe = misc * cosk + pltpu.roll(misc, LANES - QK_ROPE, 1) * sink
    for hd in range(MLA_HEADS):
        sl = slice(hd * HEAD_SLOT, (hd + 1) * HEAD_SLOT)
        qh = qa[:, sl]
        q_ref[:, sl] = (qh * cosq + pltpu.roll(qh, LANES - QK_ROPE, 1) * sinq).astype(BF16)
        k_ref[:, sl] = (kv[:, sl] + kpe).astype(BF16)
    v_ref[...] = (kv[:, ATT_W:] + vone_ref[...]).astype(BF16)


def _chunk_cumsum_mats(tm):
    i = np.arange(tm)
    same = (i[:, None] // CHUNK) == (i[None, :] // CHUNK)
    pre = (same & (i[None, :] <= i[:, None])).astype(np.float32)
    suf = (same & (i[None, :] >= i[:, None])).astype(np.float32)
    return jnp.asarray(np.tile(pre, (1, 3)), BF16), jnp.asarray(np.tile(suf, (1, 3)), BF16)


def _mod_spec(d, layer, part, per_b, fixed_row):
    if fixed_row is None:
        return pl.BlockSpec((None, None, 1, d), lambda i: (layer, i // per_b, 0, part))
    return pl.BlockSpec((None, None, 1, d), lambda i: (layer, fixed_row, 0, part))


def _in_proj(x, mod, mod_row, g, win, qg, wq, kvg, wkv, vone, tab, alog, dtb, seq, tm, layer):
    n, d = x.shape
    cump, cums = _chunk_cumsum_mats(tm)
    per_b = seq // tm
    const = lambda i: (0, 0)
    row = lambda i: (i, 0)
    weight = lambda a, b: pl.BlockSpec((None, a, b), lambda i: (layer, 0, 0), pipeline_mode=pl.Buffered(1))
    out_shapes = (jax.ShapeDtypeStruct((n, CONV_W), F32), jax.ShapeDtypeStruct((n, GDN_VAL_W), F32),
                  jax.ShapeDtypeStruct((n, LANES), F32), jax.ShapeDtypeStruct((n, ATT_W), BF16),
                  jax.ShapeDtypeStruct((n, ATT_W), BF16), jax.ShapeDtypeStruct((n, ATT_W), BF16))
    return pl.pallas_call(
        _in_kernel,
        grid=(n // tm,),
        in_specs=[pl.BlockSpec((tm, d), row),
                  _mod_spec(d, layer, 0, per_b, mod_row), _mod_spec(d, layer, 1, per_b, mod_row),
                  pl.BlockSpec((1, d), const), weight(d, IN_W),
                  pl.BlockSpec((1, Q_LORA), const), weight(Q_LORA, ATT_W),
                  pl.BlockSpec((1, KV_LORA), const), weight(KV_LORA, 2 * ATT_W),
                  pl.BlockSpec((1, ATT_W), const),
                  pl.BlockSpec((tm, 4 * LANES), lambda i: (i % per_b, 0)),
                  pl.BlockSpec((1, LANES), const), pl.BlockSpec((1, LANES), const),
                  pl.BlockSpec((tm, 3 * tm), const), pl.BlockSpec((tm, 3 * tm), const)],
        out_specs=[pl.BlockSpec((tm, CONV_W), row), pl.BlockSpec((tm, GDN_VAL_W), row),
                   pl.BlockSpec((tm, LANES), row), pl.BlockSpec((tm, ATT_W), row),
                   pl.BlockSpec((tm, ATT_W), row), pl.BlockSpec((tm, ATT_W), row)],
        out_shape=out_shapes,
        compiler_params=pltpu.CompilerParams(dimension_semantics=("arbitrary",),
                                             vmem_limit_bytes=VMEM_LIMIT),
        name="in_proj",
    )(x, mod, mod, g, win, qg, wq, kvg, wkv, vone, tab, alog, dtb, cump, cums)


def _conv_silu(x_ref, w_ref, pad_s):
    t = x_ref.shape[0]
    zero = jnp.zeros((SUBLANES, LANES), F32)
    pad_s[0:SUBLANES, :] = zero
    pad_s[SUBLANES + t:2 * SUBLANES + t, :] = zero
    pad_s[SUBLANES:SUBLANES + t, :] = x_ref[...]
    acc = None
    for j in range(CONV_K):
        off = SUBLANES - CONV_K // 2 + j
        term = pad_s[off:off + t, :] * w_ref[j:j + 1, :]
        acc = term if acc is None else acc + term
    return _silu(acc)


def _hi_lo(x):
    hi = x.astype(BF16).astype(F32)
    return hi, (x - hi).astype(BF16).astype(F32)


def _seg_sum(x, gmat2):
    hi, lo = _hi_lo(x)
    return _mm(jnp.concatenate([hi, lo], axis=1).astype(BF16), gmat2)


def _bd_rows(x, left):
    return jnp.concatenate([jnp.where(left, x, 0.0), jnp.where(left, 0.0, x)], axis=0)


def _bd_w(x, left):
    return _bd_rows(x, left).astype(BF16)


def _tri_inverse_pair(amats, lvl_mask, eye, left):
    c = amats[0].shape[0]
    n1 = [jnp.where(lvl_mask[0], -a, 0.0) for a in amats]
    n2 = [_mm(n.astype(BF16), _bd_w(n, left)) for n in n1]
    t = [eye + n for n in n1]
    y = [_mm(jnp.concatenate([x, n], axis=0).astype(BF16), _bd_w(n, left)) for x, n in zip(t, n2)]
    t = [x + yy[0:c] for x, yy in zip(t, y)]
    t = [x + _mm(x.astype(BF16), _bd_w(yy[c:2 * c], left)) for x, yy in zip(t, y)]
    for m in lvl_mask[1:]:
        tq = [_mm(x.astype(BF16), _bd_w(jnp.where(m, a, 0.0), left)) for x, a in zip(t, amats)]
        t = [x - _mm(y_.astype(BF16), _bd_w(x, left)) for x, y_ in zip(t, tq)]
    return t


def _gdn_kernel(qc_ref, kc_ref, vc_ref, gc_ref, zc_ref, ql_ref, kl_ref, vl_ref, gl_ref, zl_ref,
                wq_ref, wk_ref, wv_ref, gn_ref,
                yc_ref, yl_ref,
                q_s, k_s, v_s, gcc_s, bb_s, acc_s, pad_s):
    pair = pl.program_id(1)
    c = CHUNK
    tc = qc_ref.shape[0]
    tl = ql_ref.shape[0]
    ngate = 2 * GDN_HEADS
    ri = lax.broadcasted_iota(jnp.int32, (c, LANES), 0)
    li = lax.broadcasted_iota(jnp.int32, (c, LANES), 1)
    ci = li & (c - 1)
    left = li < c
    incl = (ri >= ci, ri <= ci)
    strict = (ri > ci, ri < ci)
    eye = (ri == ci).astype(F32)
    bi, bj = ri >> 3, ci >> 3
    lvl_mask = (bi == bj,
                ((bi >> 1) == (bj >> 1)) & (bi != bj),
                ((bi >> 2) == (bj >> 2)) & ((bi >> 1) != (bj >> 1)),
                (bi >> 2) != (bj >> 2))
    rr = lax.broadcasted_iota(jnp.int32, (LANES, LANES), 0)
    ll = lax.broadcasted_iota(jnp.int32, (LANES, LANES), 1)
    bd = (rr >> 6) == (ll >> 6)
    top = rr < c
    gmat = bd.astype(BF16)
    gmat2 = jnp.concatenate([gmat, gmat], axis=0)
    src = lax.broadcasted_iota(jnp.int32, (3 * LANES, 4 * LANES), 0) & (LANES - 1)
    col = lax.broadcasted_iota(jnp.int32, (3 * LANES, 4 * LANES), 1)
    quantity, hh = col >> 7, (col >> 6) & 1
    want = (quantity & 1) * GDN_HEADS + (quantity >> 1) * ngate + 2 * pair + hh
    expand = (src == want).astype(BF16)

    def stage(q_ref, k_ref, v_ref, g_ref, base, t):
        uq = _conv_silu(q_ref, wq_ref, pad_s)
        q_s[base:base + t, :] = uq * lax.rsqrt(_seg_sum(uq * uq, gmat2) + EPS) * (GDN_DK ** -0.5)
        uk = _conv_silu(k_ref, wk_ref, pad_s)
        k_s[base:base + t, :] = uk * lax.rsqrt(_seg_sum(uk * uk, gmat2) + EPS)
        v_s[base:base + t, :] = _conv_silu(v_ref, wv_ref, pad_s)
        out = _mm(_split3(g_ref[...], 1), expand)
        for d in range(2):
            gcc_s[d, base:base + t, :] = out[:, d * LANES:(d + 1) * LANES]
            bb_s[d, base:base + t, :] = out[:, (2 + d) * LANES:(3 + d) * LANES]

    acc_s[...] = jnp.zeros_like(acc_s)
    stage(qc_ref, kc_ref, vc_ref, gc_ref, 0, tc)
    stage(ql_ref, kl_ref, vl_ref, gl_ref, tc, tl)

    def chunk_step(streams, starts, states):
        n = range(len(streams))
        dirs = [d for d, u in streams]
        rows = [pl.ds(s, c) for s in starts]
        q = [q_s[r, :] for r in rows]
        k = [k_s[r, :] for r in rows]
        v = [v_s[r, :] for r in rows]
        gcc = [gcc_s[d, r, :] for d, r in zip(dirs, rows)]
        bb = [bb_s[d, r, :] for d, r in zip(dirs, rows)]
        grow = [jnp.sum(eye * g, axis=0, keepdims=True) for g in gcc]
        decay = [jnp.where(incl[d], jnp.exp(jnp.where(incl[d], gcc[i] - grow[i], 0.0)), 0.0)
                 for i, d in enumerate(dirs)]
        kb = [k[i] * bb[i] for i in n]
        kt = [jnp.transpose(k[i]) for i in n]
        kbd = [jnp.where(bd, jnp.concatenate([kt[i], kt[i]], axis=1), 0.0).astype(BF16) for i in n]
        pm = [_mm(jnp.concatenate([kb[i], q[i]], axis=0).astype(BF16), kbd[i]) for i in n]
        amat = [jnp.where(strict[d], pm[i][0:c] * decay[i], 0.0) for i, d in enumerate(dirs)]
        qk = [(pm[i][c:2 * c] * decay[i]).astype(BF16) for i in n]
        tinv = _tri_inverse_pair(amat, lvl_mask, eye, left)
        egc = [jnp.exp(g) for g in gcc]
        gl = [gcc[i][c - 1:c, :] if d == 0 else gcc[i][0:1, :] for i, d in enumerate(dirs)]
        rhs = [jnp.concatenate([_bd_rows(kb[i] * egc[i], left), _bd_rows(v[i] * bb[i], left)], axis=1).astype(BF16)
               for i in n]
        wu = [_mm(tinv[i].astype(BF16), rhs[i]) for i in n]
        wub = [x.astype(BF16) for x in wu]
        wbd = [jnp.concatenate([_bd_rows(wu[i][:, 0:LANES], left), _bd_rows(wu[i][:, LANES:], left)],
                               axis=1).astype(BF16) for i in n]
        kscale = [jnp.exp(gl[i] - grow[i]) for i in n]
        kgt = [(kt[i] * jnp.where(top[:, 0:c], kscale[i][:, 0:c], kscale[i][:, c:])).astype(BF16) for i in n]
        qkwu = [_mm(qk[i], wbd[i]) for i in n]
        kgwu = [_mm(kgt[i], wub[i]) for i in n]
        lhs = [jnp.concatenate([jnp.where(bd, -kgwu[i][:, 0:LANES], 0.0),
                                q[i] * egc[i] - qkwu[i][:, 0:LANES]], axis=0).astype(BF16) for i in n]
        nmat = [jnp.where(bd, kgwu[i][:, LANES:], 0.0) for i in n]
        egl = [jnp.exp(g) for g in gl]
        st = list(states)
        for u in sorted({u for d, u in streams}):
            sel = [i for i in n if streams[i][1] == u]
            mqs = {i: _mm(lhs[i], st[dirs[i]].astype(BF16)) for i in sel}
            for i in sel:
                d = dirs[i]
                st[d] = st[d] * egl[i] + mqs[i][0:LANES] + nmat[i]
                acc_s[rows[i], :] = acc_s[rows[i], :] + qkwu[i][:, LANES:] + mqs[i][LANES:]
        return tuple(st)

    def run(base, nchunks, nsub, states):
        streams = [(d, u) for u in range(nsub) for d in range(2)]

        def body(i, st):
            pos = [i * nsub + u for d, u in streams]
            idx = [p if d == 0 else nchunks - 1 - p for p, (d, u) in zip(pos, streams)]
            return chunk_step(streams, [pl.multiple_of(base + x * c, c) for x in idx], st)
        return lax.fori_loop(0, nchunks // nsub, body, states)

    zero = jnp.zeros((LANES, LANES), F32)
    states = run(0, tc // c, math.gcd(tc // c, GDN_SUB), (zero, zero))
    run(tc, tl // c, math.gcd(tl // c, GDN_SUB), states)

    def finish(z_ref, y_ref, base, t):
        o = acc_s[base:base + t, :]
        ms = _seg_sum(o * o, gmat2) * (1.0 / GDN_DV)
        y_ref[...] = (o * lax.rsqrt(ms + EPS) * gn_ref[...] * _silu(z_ref[...])).astype(y_ref.dtype)

    finish(zc_ref, yc_ref, 0, tc)
    finish(zl_ref, yl_ref, tc, tl)


def _gdn(qkv_c, gate_c, z_c, qkv_l, gate_l, z_l, conv_w, gn2, batch, tc, tl):
    npair = GDN_HEADS // 2
    kq, kk, kvv = 0, npair, 2 * npair
    blk = lambda t, off: pl.BlockSpec((t, LANES), lambda b, p: (b, off + p))
    mblk = lambda t: pl.BlockSpec((t, LANES), lambda b, p: (b, 0))
    wblk = lambda off: pl.BlockSpec((CONV_K, LANES), lambda b, p: (0, off + p))
    const = lambda b, p: (0, 0)
    tt = tc + tl
    seq_buf = pltpu.VMEM((tt, LANES), F32)
    return pl.pallas_call(
        _gdn_kernel,
        grid=(batch, npair),
        in_specs=[blk(tc, kq), blk(tc, kk), blk(tc, kvv), mblk(tc), blk(tc, 0),
                  blk(tl, kq), blk(tl, kk), blk(tl, kvv), mblk(tl), blk(tl, 0),
                  wblk(kq), wblk(kk), wblk(kvv),
                  pl.BlockSpec((1, LANES), const)],
        out_specs=[blk(tc, 0), blk(tl, 0)],
        out_shape=(jax.ShapeDtypeStruct((batch * tc, GDN_VAL_W), BF16),
                   jax.ShapeDtypeStruct((batch * tl, GDN_VAL_W), BF16)),
        scratch_shapes=[seq_buf, seq_buf, seq_buf,
                        pltpu.VMEM((2, tt, LANES), F32), pltpu.VMEM((2, tt, LANES), F32), seq_buf,
                        pltpu.VMEM((max(tc, tl) + 2 * SUBLANES, LANES), F32)],
        compiler_params=pltpu.CompilerParams(dimension_semantics=("arbitrary", "arbitrary"),
                                             vmem_limit_bytes=VMEM_LIMIT),
        name="gdn",
    )(qkv_c, qkv_c, qkv_c, gate_c, z_c, qkv_l, qkv_l, qkv_l, gate_l, z_l,
      conv_w, conv_w, conv_w, gn2)


def _attn_kernel(*refs, n_src):
    q_ref = refs[0]
    k_refs = refs[1:1 + n_src]
    v_refs = refs[1 + n_src:1 + 2 * n_src]
    o_ref = refs[1 + 2 * n_src]
    heads = range(2)
    sls = [slice(j * HEAD_SLOT, (j + 1) * HEAD_SLOT) for j in heads]
    scores = [[lax.dot_general(q_ref[:, sl], kr[:, sl], (((1,), (1,)), ((), ())), preferred_element_type=F32)
               for kr in k_refs] for sl in sls]
    for j in heads:
        m = None
        for s in scores[j]:
            ms = jnp.max(s, axis=-1, keepdims=True)
            m = ms if m is None else jnp.maximum(m, ms)
        acc = None
        for s, vr in zip(scores[j], v_refs):
            p = jnp.exp(s - m).astype(BF16)
            t = _mm(p, vr[:, sls[j]])
            acc = t if acc is None else acc + t
        out = acc[:, 0:V_DIM] / acc[:, V_DIM:V_DIM + 1]
        o_ref[:, j * V_DIM:(j + 1) * V_DIM] = out.astype(o_ref.dtype)


def _attention(q, ks, vs, batch, tq_total, tq, src_lens):
    n_src = len(ks)
    nq = tq_total // tq
    npair = MLA_HEADS // 2
    qmap = lambda b, p, i: (b * nq + i, p)
    kvmap = lambda b, p, i: (b, p)
    in_specs = [pl.BlockSpec((tq, 2 * HEAD_SLOT), qmap)]
    in_specs += [pl.BlockSpec((t, 2 * HEAD_SLOT), kvmap) for t in src_lens]
    in_specs += [pl.BlockSpec((t, 2 * HEAD_SLOT), kvmap) for t in src_lens]
    return pl.pallas_call(
        functools.partial(_attn_kernel, n_src=n_src),
        grid=(batch, npair, nq),
        in_specs=in_specs,
        out_specs=pl.BlockSpec((tq, 2 * V_DIM), qmap),
        out_shape=jax.ShapeDtypeStruct((batch * tq_total, MLA_HEADS * V_DIM), BF16),
        compiler_params=pltpu.CompilerParams(dimension_semantics=("arbitrary", "arbitrary", "arbitrary"),
                                             vmem_limit_bytes=VMEM_LIMIT),
        name="attention",
    )(q, *ks, *vs)


def _out_ffn_kernel(x_ref, yg_ref, ym_ref, wo_ref, g1_ref, sh2_ref, sc2_ref, g2_ref, n2_ref,
                    wg_ref, wu_ref, wd_ref, fn_ref, o_ref, *, final_norm, th):
    half = GDN_VAL_W
    y = _mm(yg_ref[...], wo_ref[0:half, :]) + _mm(ym_ref[...], wo_ref[half:, :])
    x1 = x_ref[...] + g1_ref[...] * y
    h2 = (_rms(x1, n2_ref[...]) * (1.0 + sc2_ref[...]) + sh2_ref[...]).astype(BF16)
    acc = None
    for j in range(FFN_HIDDEN // th):
        hs = slice(j * th, (j + 1) * th)
        act = (_silu(_mm(h2, wg_ref[:, hs])) * _mm(h2, wu_ref[:, hs])).astype(BF16)
        part = _mm(act, wd_ref[hs, :])
        acc = part if acc is None else acc + part
    x2 = x1 + g2_ref[...] * acc
    if final_norm:
        x2 = _rms(x2, fn_ref[...])
    o_ref[...] = x2


def _out_ffn(x, yg, ym, wo, mod, mod_row, n2, wg, wu, wd, fn, seq, tm, th, final_norm, layer):
    n, d = x.shape
    per_b = seq // tm
    const = lambda i: (0, 0)
    row = lambda i: (i, 0)
    hid = FFN_HIDDEN
    mod_specs = [_mod_spec(d, layer, part, per_b, mod_row) for part in (2, 3, 4, 5)]
    resident = lambda shape: pl.BlockSpec((None,) + shape, lambda i: (layer, 0, 0),
                                          pipeline_mode=pl.Buffered(1))
    return pl.pallas_call(
        functools.partial(_out_ffn_kernel, final_norm=final_norm, th=th),
        grid=(n // tm,),
        in_specs=[pl.BlockSpec((tm, d), row),
                  pl.BlockSpec((tm, GDN_VAL_W), row), pl.BlockSpec((tm, MLA_HEADS * V_DIM), row),
                  resident((d, d)),
                  *mod_specs,
                  pl.BlockSpec((1, d), const),
                  resident((d, hid)), resident((d, hid)), resident((hid, d)),
                  pl.BlockSpec((1, d), const)],
        out_specs=pl.BlockSpec((tm, d), row),
        out_shape=jax.ShapeDtypeStruct((n, d), F32),
        compiler_params=pltpu.CompilerParams(dimension_semantics=("arbitrary",),
                                             vmem_limit_bytes=VMEM_LIMIT),
        name="out_ffn",
    )(x, yg, ym, wo, mod, mod, mod, mod, n2, wg, wu, wd, fn)


_ROT_PERM = np.concatenate([np.arange(8, 16), np.arange(0, 8), np.arange(24, 32), np.arange(16, 24)])
_ROT_SIGN = np.concatenate([-np.ones(8), np.ones(8), -np.ones(8), np.ones(8)]).astype(np.float32)


def _rot_cols(w):
    return w[..., _ROT_PERM] * _ROT_SIGN


def _arrange_w_in(w_in):
    o = np.cumsum([0, CONV_W, GDN_VAL_W, 2 * GDN_HEADS, 2 * GDN_HEADS, Q_LORA, KV_LORA, QK_ROPE])
    qkv, z, a, b, cq, ckv, kr = (w_in[..., o[i]:o[i + 1]] for i in range(7))
    pad = jnp.zeros(w_in.shape[:-1] + (LANES - 4 * GDN_HEADS - 2 * QK_ROPE,), w_in.dtype)
    return jnp.concatenate([qkv, z, cq, ckv, a, b, pad, kr, _rot_cols(kr)], axis=-1).astype(BF16)


def _arrange_w_uq(w_uq):
    lead = w_uq.shape[:-1]
    w = w_uq.reshape(lead + (MLA_HEADS, QK_NOPE + QK_ROPE))
    pe = w[..., QK_NOPE:]
    return jnp.concatenate([w[..., :QK_NOPE], pe, _rot_cols(pe)], axis=-1).reshape(lead + (ATT_W,)).astype(BF16)


def _arrange_w_ukv(w_ukv):
    lead = w_ukv.shape[:-1]
    w = w_ukv.reshape(lead + (MLA_HEADS, QK_NOPE + V_DIM))
    zk = jnp.zeros(lead + (MLA_HEADS, HEAD_SLOT - QK_NOPE), w.dtype)
    zv = jnp.zeros(lead + (MLA_HEADS, HEAD_SLOT - V_DIM), w.dtype)
    wk = jnp.concatenate([w[..., :QK_NOPE], zk], axis=-1).reshape(lead + (ATT_W,))
    wv = jnp.concatenate([w[..., QK_NOPE:], zv], axis=-1).reshape(lead + (ATT_W,))
    return jnp.concatenate([wk, wv], axis=-1).astype(BF16)


def _rope_tables(t_lat, t_ctx):
    rows = t_lat // GRID_W
    row = np.repeat(np.arange(rows), GRID_W).astype(np.float64)
    col = np.tile(np.arange(GRID_W), rows).astype(np.float64)
    inv_freq = ROPE_THETA ** (-np.arange(0, AXIS_DIM, 2, dtype=np.float64) / AXIS_DIM)

    def axis_angles(pos):
        ang = pos[:, None] * inv_freq[None, :]
        return np.concatenate([ang, ang], axis=-1)

    ang = np.concatenate([axis_angles(row), axis_angles(col)], axis=-1)
    scale = (QK_NOPE + QK_ROPE) ** -0.5

    def build(cos_t, sin_t):
        t = cos_t.shape[0]
        z32 = np.zeros((t, QK_ROPE))
        z64 = np.zeros((t, QK_NOPE))
        cosq = np.concatenate([np.ones((t, QK_NOPE)), cos_t, z32], axis=1) * scale
        sinq = np.concatenate([z64, sin_t, z32], axis=1) * scale
        cosk = np.concatenate([z64, cos_t, z32], axis=1)
        sink = np.concatenate([z64, sin_t, z32], axis=1)
        return jnp.asarray(np.concatenate([cosq, sinq, cosk, sink], axis=1), F32)

    lat = build(np.cos(ang), np.sin(ang))
    ctx = build(np.ones((t_ctx, QK_ROPE)), np.zeros((t_ctx, QK_ROPE)))
    return lat, ctx


def kernel(x, c, ctx, c_ctx, w_mod, b_mod, norm1_g, norm2_g, w_in, conv_w, a_log, dt_bias, gdn_norm_g,
           q_norm_g, w_uq, kv_norm_g, w_ukv, w_out, w_gate, w_up, w_down, final_norm_g):
    batch, seq, d = x.shape
    tctx = ctx.shape[1]
    tab_l, tab_c = _rope_tables(seq, tctx)
    vone_np = np.zeros((MLA_HEADS, HEAD_SLOT), np.float32)
    vone_np[:, V_DIM] = 1.0
    vone = jnp.asarray(vone_np.reshape(1, ATT_W))
    mod_rows = 2 * SUBLANES
    ctx_row = batch
    cc = jnp.concatenate([c, c_ctx[None, :], jnp.zeros((mod_rows - batch - 1, d), F32)], axis=0)
    xs = x.reshape(batch * seq, d)
    cs = ctx.reshape(batch * tctx, d)
    tm_l, tm_c, th = 256, 256, MXU_WIDTH

    mod = _modulation(cc, w_mod, b_mod[:, None, :]).reshape(DEPTH, mod_rows, 1, 6 * d)
    win, wq, wkv = _arrange_w_in(w_in), _arrange_w_uq(w_uq), _arrange_w_ukv(w_ukv)
    wo, wg, wu, wd = w_out.astype(BF16), w_gate.astype(BF16), w_up.astype(BF16), w_down.astype(BF16)
    gate_pad = ((0, 0), (0, LANES - 2 * GDN_HEADS))
    alog_all = jnp.pad(a_log.reshape(DEPTH, -1), gate_pad)
    dtb_all = jnp.pad(dt_bias.reshape(DEPTH, -1), gate_pad)
    gn2_all = jnp.tile(gdn_norm_g, (1, 2))
    fn = final_norm_g[None, :]

    for l in range(DEPTH):
        last = l == DEPTH - 1
        n1, n2 = norm1_g[l:l + 1], norm2_g[l:l + 1]
        qg, kvg = q_norm_g[l:l + 1], kv_norm_g[l:l + 1]
        alog, dtb = alog_all[l:l + 1], dtb_all[l:l + 1]
        qkv_c, z_c, gate_c, q_c, k_c, v_c = _in_proj(cs, mod, ctx_row, n1, win, qg, wq, kvg, wkv, vone,
                                                     tab_c, alog, dtb, tctx, tm_c, l)
        qkv_l, z_l, gate_l, q_l, k_l, v_l = _in_proj(xs, mod, None, n1, win, qg, wq, kvg, wkv, vone,
                                                     tab_l, alog, dtb, seq, tm_l, l)
        yg_c, yg_l = _gdn(qkv_c, gate_c, z_c, qkv_l, gate_l, z_l, conv_w[l], gn2_all[l:l + 1],
                          batch, tctx, seq)
        ym_l = _attention(q_l, (k_c, k_l), (v_c, v_l), batch, seq, min(seq, 1024), (tctx, seq))
        xs = _out_ffn(xs, yg_l, ym_l, wo, mod, None, n2, wg, wu, wd, fn, seq, tm_l * 2, th, last, l)
        if not last:
            ym_c = _attention(q_c, (k_c,), (v_c,), batch, tctx, tctx, (tctx,))
            cs = _out_ffn(cs, yg_c, ym_c, wo, mod, ctx_row, n2, wg, wu, wd, fn, tctx, tm_c, th, False, l)
    return xs.reshape(batch, seq, d)
```

```python
import functools
import math

import jax
import jax.numpy as jnp
import numpy as np
from jax import lax
from jax.experimental import pallas as pl
from jax.experimental.pallas import tpu as pltpu

D_MODEL = 1024
DEPTH = 2
GRID_W = 64
EPS = 1e-6

GDN_HEADS = 8
GDN_DK = 64
GDN_DV = 64
GDN_KEY_W = GDN_HEADS * GDN_DK
GDN_VAL_W = GDN_HEADS * GDN_DV
CONV_W = 2 * GDN_KEY_W + GDN_VAL_W
CONV_K = 5
CHUNK = 64
GDN_SUB = 8

MLA_HEADS = 8
QK_NOPE = 64
QK_ROPE = 32
V_DIM = 64
Q_LORA = 256
KV_LORA = 128
ROPE_THETA = 10000.0
AXIS_DIM = QK_ROPE // 2
FFN_HIDDEN = -(-8 * D_MODEL // (3 * 256)) * 256

LANES = 128
SUBLANES = 8
MXU_WIDTH = 256
HEAD_SLOT = LANES
ATT_W = MLA_HEADS * HEAD_SLOT
IN_W = CONV_W + GDN_VAL_W + Q_LORA + KV_LORA + LANES
VMEM_LIMIT = 56 * 1024 * 1024

TM_IN = 256
TM_FFN = 512
TQ_ATT = 1024
FFN_TILE = MXU_WIDTH

F32 = jnp.float32
BF16 = jnp.bfloat16


def _sigmoid(x):
    return 0.5 * (1.0 + jnp.tanh(0.5 * x))


def _silu(x):
    return x * _sigmoid(x)


def _rms(x, g):
    return x * lax.rsqrt(jnp.mean(x * x, axis=-1, keepdims=True) + EPS) * g


def _mm(a, b):
    return jnp.dot(a, b, preferred_element_type=F32)


def _mod_kernel(c_ref, w_ref, b_ref, o_ref):
    o_ref[...] = _mm(_silu(c_ref[...]).astype(BF16), w_ref[...].astype(BF16)) + b_ref[...]


def _modulation(cc, w_mod, b_mod):
    rows, d = cc.shape
    depth, _, n = w_mod.shape
    tn = 1024
    return pl.pallas_call(
        _mod_kernel,
        grid=(depth, n // tn),
        in_specs=[pl.BlockSpec((rows, d), lambda l, j: (0, 0)),
                  pl.BlockSpec((None, d, tn), lambda l, j: (l, 0, j)),
                  pl.BlockSpec((None, 1, tn), lambda l, j: (l, 0, j))],
        out_specs=pl.BlockSpec((None, rows, tn), lambda l, j: (l, 0, j)),
        out_shape=jax.ShapeDtypeStruct((depth, rows, n), F32),
        compiler_params=pltpu.CompilerParams(dimension_semantics=("arbitrary", "arbitrary"),
                                             vmem_limit_bytes=VMEM_LIMIT),
        name="modulation",
    )(cc, w_mod, b_mod)


def _split3(x, axis):
    hi = x.astype(BF16).astype(F32)
    mid = (x - hi).astype(BF16).astype(F32)
    return jnp.concatenate([hi, mid, x - hi - mid], axis=axis).astype(BF16)


def _in_kernel(x_ref, sh_ref, sc_ref, g_ref, win_ref, qg_ref, wq_ref, kvg_ref, wkv_ref, vone_ref, tab_ref,
               alog_ref, dtb_ref, cump_ref, cums_ref,
               qkv_ref, z_ref, gate_ref, q_ref, k_ref, v_ref):
    x = x_ref[...]
    h = _rms(x, g_ref[...]) * (1.0 + sc_ref[...]) + sh_ref[...]
    p = _mm(h.astype(BF16), win_ref[...])
    o1 = CONV_W
    o2 = o1 + GDN_VAL_W
    o3 = o2 + Q_LORA
    o4 = o3 + KV_LORA
    qkv_ref[...] = p[:, :o1]
    z_ref[...] = p[:, o1:o2].astype(BF16)
    misc = p[:, o4:]
    lane = lax.broadcasted_iota(jnp.int32, (1, LANES), 1)
    ngate = 2 * GDN_HEADS
    a = misc + dtb_ref[...]
    gdec = -jnp.exp(alog_ref[...]) * (jnp.maximum(a, 0.0) + jnp.log(1.0 + jnp.exp(-jnp.abs(a))))
    g3 = _split3(jnp.where(lane < ngate, gdec, 0.0), 0)
    gsum = jnp.where(lane < GDN_HEADS, _mm(cump_ref[...], g3), _mm(cums_ref[...], g3))
    gate_ref[...] = jnp.where(lane < ngate, gsum, jnp.where(lane < 2 * ngate, _sigmoid(misc), 0.0))
    cqn = _rms(p[:, o2:o3], qg_ref[...]).astype(BF16)
    ckvn = _rms(p[:, o3:o4], kvg_ref[...]).astype(BF16)
    qa = _mm(cqn, wq_ref[...])
    kv = _mm(ckvn, wkv_ref[...])
    tab = tab_ref[...]
    cosq, sinq = tab[:, 0:LANES], tab[:, LANES:2 * LANES]
    cosk, sink = tab[:, 2 * LANES:3 * LANES], tab[:, 3 * LANES:4 * LANES]
    kpe = misc * cosk + pltpu.roll(misc, LANES - QK_ROPE, 1) * sink
    for hd in range(MLA_HEADS):
        sl = slice(hd * HEAD_SLOT, (hd + 1) * HEAD_SLOT)
        qh = qa[:, sl]
        q_ref[:, sl] = (qh * cosq + pltpu.roll(qh, LANES - QK_ROPE, 1) * sinq).astype(BF16)
        k_ref[:, sl] = (kv[:, sl] + kpe).astype(BF16)
    v_ref[...] = (kv[:, ATT_W:] + vone_ref[...]).astype(BF16)


def _chunk_cumsum_mats(tm):
    i = np.arange(tm)
    same = (i[:, None] // CHUNK) == (i[None, :] // CHUNK)
    pre = (same & (i[None, :] <= i[:, None])).astype(np.float32)
    suf = (same & (i[None, :] >= i[:, None])).astype(np.float32)
    return jnp.asarray(np.tile(pre, (1, 3)), BF16), jnp.asarray(np.tile(suf, (1, 3)), BF16)


def _mod_spec(d, layer, part, per_b, fixed_row):
    if fixed_row is None:
        return pl.BlockSpec((None, None, 1, d), lambda i: (layer, i // per_b, 0, part))
    return pl.BlockSpec((None, None, 1, d), lambda i: (layer, fixed_row, 0, part))


def _in_proj(x, mod, mod_row, g, win, qg, wq, kvg, wkv, vone, tab, alog, dtb, seq, tm, layer):
    n, d = x.shape
    cump, cums = _chunk_cumsum_mats(tm)
    per_b = seq // tm
    const = lambda i: (0, 0)
    row = lambda i: (i, 0)
    weight = lambda a, b: pl.BlockSpec((None, a, b), lambda i: (layer, 0, 0), pipeline_mode=pl.Buffered(1))
    out_shapes = (jax.ShapeDtypeStruct((n, CONV_W), F32), jax.ShapeDtypeStruct((n, GDN_VAL_W), BF16),
                  jax.ShapeDtypeStruct((n, LANES), F32), jax.ShapeDtypeStruct((n, ATT_W), BF16),
                  jax.ShapeDtypeStruct((n, ATT_W), BF16), jax.ShapeDtypeStruct((n, ATT_W), BF16))
    return pl.pallas_call(
        _in_kernel,
        grid=(n // tm,),
        in_specs=[pl.BlockSpec((tm, d), row),
                  _mod_spec(d, layer, 0, per_b, mod_row), _mod_spec(d, layer, 1, per_b, mod_row),
                  pl.BlockSpec((1, d), const), weight(d, IN_W),
                  pl.BlockSpec((1, Q_LORA), const), weight(Q_LORA, ATT_W),
                  pl.BlockSpec((1, KV_LORA), const), weight(KV_LORA, 2 * ATT_W),
                  pl.BlockSpec((1, ATT_W), const),
                  pl.BlockSpec((tm, 4 * LANES), lambda i: (i % per_b, 0)),
                  pl.BlockSpec((1, LANES), const), pl.BlockSpec((1, LANES), const),
                  pl.BlockSpec((tm, 3 * tm), const), pl.BlockSpec((tm, 3 * tm), const)],
        out_specs=[pl.BlockSpec((tm, CONV_W), row), pl.BlockSpec((tm, GDN_VAL_W), row),
                   pl.BlockSpec((tm, LANES), row), pl.BlockSpec((tm, ATT_W), row),
                   pl.BlockSpec((tm, ATT_W), row), pl.BlockSpec((tm, ATT_W), row)],
        out_shape=out_shapes,
        compiler_params=pltpu.CompilerParams(dimension_semantics=("arbitrary",),
                                             vmem_limit_bytes=VMEM_LIMIT),
        name="in_proj",
    )(x, mod, mod, g, win, qg, wq, kvg, wkv, vone, tab, alog, dtb, cump, cums)


def _conv_silu(x_ref, w_ref, pad_s):
    t = x_ref.shape[0]
    zero = jnp.zeros((SUBLANES, LANES), F32)
    pad_s[0:SUBLANES, :] = zero
    pad_s[SUBLANES + t:2 * SUBLANES + t, :] = zero
    pad_s[SUBLANES:SUBLANES + t, :] = x_ref[...]
    acc = None
    for j in range(CONV_K):
        off = SUBLANES - CONV_K // 2 + j
        term = pad_s[off:off + t, :] * w_ref[j:j + 1, :]
        acc = term if acc is None else acc + term
    return _silu(acc)


def _hi_lo(x):
    hi = x.astype(BF16).astype(F32)
    return hi, (x - hi).astype(BF16).astype(F32)


def _seg_sum(x, gmat2):
    hi, lo = _hi_lo(x)
    return _mm(jnp.concatenate([hi, lo], axis=1).astype(BF16), gmat2)


def _bd_rows(x, left):
    return jnp.concatenate([jnp.where(left, x, 0.0), jnp.where(left, 0.0, x)], axis=0)


def _bd_w(x, left):
    return _bd_rows(x, left).astype(BF16)


def _tri_inverse_pair(amats, lvl_mask, eye, left):
    c = amats[0].shape[0]
    n1 = [jnp.where(lvl_mask[0], -a, 0.0) for a in amats]
    n2 = [_mm(n.astype(BF16), _bd_w(n, left)) for n in n1]
    t = [eye + n for n in n1]
    y = [_mm(jnp.concatenate([x, n], axis=0).astype(BF16), _bd_w(n, left)) for x, n in zip(t, n2)]
    t = [x + yy[0:c] for x, yy in zip(t, y)]
    t = [x + _mm(x.astype(BF16), _bd_w(yy[c:2 * c], left)) for x, yy in zip(t, y)]
    for m in lvl_mask[1:]:
        tq = [_mm(x.astype(BF16), _bd_w(jnp.where(m, a, 0.0), left)) for x, a in zip(t, amats)]
        t = [x - _mm(y_.astype(BF16), _bd_w(x, left)) for x, y_ in zip(t, tq)]
    return t


def _gdn_kernel(qc_ref, kc_ref, vc_ref, gc_ref, zc_ref, ql_ref, kl_ref, vl_ref, gl_ref, zl_ref,
                wq_ref, wk_ref, wv_ref, gn_ref,
                yc_ref, yl_ref,
                q_s, k_s, v_s, gcc_s, bb_s, acc_s, pad_s):
    pair = pl.program_id(1)
    c = CHUNK
    tc = qc_ref.shape[0]
    tl = ql_ref.shape[0]
    ngate = 2 * GDN_HEADS
    ri = lax.broadcasted_iota(jnp.int32, (c, LANES), 0)
    li = lax.broadcasted_iota(jnp.int32, (c, LANES), 1)
    ci = li & (c - 1)
    left = li < c
    incl = (ri >= ci, ri <= ci)
    strict = (ri > ci, ri < ci)
    eye = (ri == ci).astype(F32)
    bi, bj = ri >> 3, ci >> 3
    lvl_mask = (bi == bj,
                ((bi >> 1) == (bj >> 1)) & (bi != bj),
                ((bi >> 2) == (bj >> 2)) & ((bi >> 1) != (bj >> 1)),
                (bi >> 2) != (bj >> 2))
    rr = lax.broadcasted_iota(jnp.int32, (LANES, LANES), 0)
    ll = lax.broadcasted_iota(jnp.int32, (LANES, LANES), 1)
    bd = (rr >> 6) == (ll >> 6)
    top = rr < c
    gmat = bd.astype(BF16)
    gmat2 = jnp.concatenate([gmat, gmat], axis=0)
    src = lax.broadcasted_iota(jnp.int32, (3 * LANES, 4 * LANES), 0) & (LANES - 1)
    col = lax.broadcasted_iota(jnp.int32, (3 * LANES, 4 * LANES), 1)
    quantity, hh = col >> 7, (col >> 6) & 1
    want = (quantity & 1) * GDN_HEADS + (quantity >> 1) * ngate + 2 * pair + hh
    expand = (src == want).astype(BF16)

    def stage(q_ref, k_ref, v_ref, g_ref, base, t):
        uq = _conv_silu(q_ref, wq_ref, pad_s)
        q_s[base:base + t, :] = uq * lax.rsqrt(_seg_sum(uq * uq, gmat2) + EPS) * (GDN_DK ** -0.5)
        uk = _conv_silu(k_ref, wk_ref, pad_s)
        k_s[base:base + t, :] = uk * lax.rsqrt(_seg_sum(uk * uk, gmat2) + EPS)
        v_s[base:base + t, :] = _conv_silu(v_ref, wv_ref, pad_s)
        out = _mm(_split3(g_ref[...], 1), expand)
        for d in range(2):
            gcc_s[d, base:base + t, :] = out[:, d * LANES:(d + 1) * LANES]
            bb_s[d, base:base + t, :] = out[:, (2 + d) * LANES:(3 + d) * LANES]

    acc_s[...] = jnp.zeros_like(acc_s)
    stage(qc_ref, kc_ref, vc_ref, gc_ref, 0, tc)
    stage(ql_ref, kl_ref, vl_ref, gl_ref, tc, tl)

    def chunk_step(streams, starts, states):
        n = range(len(streams))
        dirs = [d for d, u in streams]
        rows = [pl.ds(s, c) for s in starts]
        q = [q_s[r, :] for r in rows]
        k = [k_s[r, :] for r in rows]
        v = [v_s[r, :] for r in rows]
        gcc = [gcc_s[d, r, :] for d, r in zip(dirs, rows)]
        bb = [bb_s[d, r, :] for d, r in zip(dirs, rows)]
        grow = [jnp.sum(eye * g, axis=0, keepdims=True) for g in gcc]
        decay = [jnp.where(incl[d], jnp.exp(jnp.where(incl[d], gcc[i] - grow[i], 0.0)), 0.0)
                 for i, d in enumerate(dirs)]
        kb = [k[i] * bb[i] for i in n]
        kt = [jnp.transpose(k[i]) for i in n]
        kbd = [jnp.where(bd, jnp.concatenate([kt[i], kt[i]], axis=1), 0.0).astype(BF16) for i in n]
        pm = [_mm(jnp.concatenate([kb[i], q[i]], axis=0).astype(BF16), kbd[i]) for i in n]
        amat = [jnp.where(strict[d], pm[i][0:c] * decay[i], 0.0) for i, d in enumerate(dirs)]
        qk = [(pm[i][c:2 * c] * decay[i]).astype(BF16) for i in n]
        tinv = _tri_inverse_pair(amat, lvl_mask, eye, left)
        egc = [jnp.exp(g) for g in gcc]
        gl = [gcc[i][c - 1:c, :] if d == 0 else gcc[i][0:1, :] for i, d in enumerate(dirs)]
        rhs = [jnp.concatenate([_bd_rows(kb[i] * egc[i], left), _bd_rows(v[i] * bb[i], left)], axis=1).astype(BF16)
               for i in n]
        wu = [_mm(tinv[i].astype(BF16), rhs[i]) for i in n]
        wub = [x.astype(BF16) for x in wu]
        wbd = [jnp.concatenate([_bd_rows(wu[i][:, 0:LANES], left), _bd_rows(wu[i][:, LANES:], left)],
                               axis=1).astype(BF16) for i in n]
        kscale = [jnp.exp(gl[i] - grow[i]) for i in n]
        kgt = [(kt[i] * jnp.where(top[:, 0:c], kscale[i][:, 0:c], kscale[i][:, c:])).astype(BF16) for i in n]
        qkwu = [_mm(qk[i], wbd[i]) for i in n]
        kgwu = [_mm(kgt[i], wub[i]) for i in n]
        lhs = [jnp.concatenate([jnp.where(bd, -kgwu[i][:, 0:LANES], 0.0),
                                q[i] * egc[i] - qkwu[i][:, 0:LANES]], axis=0).astype(BF16) for i in n]
        nmat = [jnp.where(bd, kgwu[i][:, LANES:], 0.0) for i in n]
        egl = [jnp.exp(g) for g in gl]
        st = list(states)
        for u in sorted({u for d, u in streams}):
            sel = [i for i in n if streams[i][1] == u]
            mqs = {i: _mm(lhs[i], st[dirs[i]].astype(BF16)) for i in sel}
            for i in sel:
                d = dirs[i]
                st[d] = st[d] * egl[i] + mqs[i][0:LANES] + nmat[i]
                acc_s[rows[i], :] = acc_s[rows[i], :] + qkwu[i][:, LANES:] + mqs[i][LANES:]
        return tuple(st)

    def run(base, nchunks, nsub, states):
        streams = [(d, u) for u in range(nsub) for d in range(2)]

        def body(i, st):
            pos = [i * nsub + u for d, u in streams]
            idx = [p if d == 0 else nchunks - 1 - p for p, (d, u) in zip(pos, streams)]
            return chunk_step(streams, [pl.multiple_of(base + x * c, c) for x in idx], st)
        return lax.fori_loop(0, nchunks // nsub, body, states)

    zero = jnp.zeros((LANES, LANES), F32)
    states = run(0, tc // c, math.gcd(tc // c, GDN_SUB), (zero, zero))
    run(tc, tl // c, math.gcd(tl // c, GDN_SUB), states)

    def finish(z_ref, y_ref, base, t):
        o = acc_s[base:base + t, :]
        ms = _seg_sum(o * o, gmat2) * (1.0 / GDN_DV)
        y_ref[...] = (o * lax.rsqrt(ms + EPS) * gn_ref[...] * _silu(z_ref[...].astype(F32))).astype(y_ref.dtype)

    finish(zc_ref, yc_ref, 0, tc)
    finish(zl_ref, yl_ref, tc, tl)


def _gdn(qkv_c, gate_c, z_c, qkv_l, gate_l, z_l, conv_w, gn2, batch, tc, tl):
    npair = GDN_HEADS // 2
    kq, kk, kvv = 0, npair, 2 * npair
    blk = lambda t, off: pl.BlockSpec((t, LANES), lambda b, p: (b, off + p))
    mblk = lambda t: pl.BlockSpec((t, LANES), lambda b, p: (b, 0))
    wblk = lambda off: pl.BlockSpec((CONV_K, LANES), lambda b, p: (0, off + p))
    const = lambda b, p: (0, 0)
    tt = tc + tl
    seq_buf = pltpu.VMEM((tt, LANES), F32)
    return pl.pallas_call(
        _gdn_kernel,
        grid=(batch, npair),
        in_specs=[blk(tc, kq), blk(tc, kk), blk(tc, kvv), mblk(tc), blk(tc, 0),
                  blk(tl, kq), blk(tl, kk), blk(tl, kvv), mblk(tl), blk(tl, 0),
                  wblk(kq), wblk(kk), wblk(kvv),
                  pl.BlockSpec((1, LANES), const)],
        out_specs=[blk(tc, 0), blk(tl, 0)],
        out_shape=(jax.ShapeDtypeStruct((batch * tc, GDN_VAL_W), BF16),
                   jax.ShapeDtypeStruct((batch * tl, GDN_VAL_W), BF16)),
        scratch_shapes=[seq_buf, seq_buf, seq_buf,
                        pltpu.VMEM((2, tt, LANES), F32), pltpu.VMEM((2, tt, LANES), F32), seq_buf,
                        pltpu.VMEM((max(tc, tl) + 2 * SUBLANES, LANES), F32)],
        compiler_params=pltpu.CompilerParams(dimension_semantics=("arbitrary", "arbitrary"),
                                             vmem_limit_bytes=VMEM_LIMIT),
        name="gdn",
    )(qkv_c, qkv_c, qkv_c, gate_c, z_c, qkv_l, qkv_l, qkv_l, gate_l, z_l,
      conv_w, conv_w, conv_w, gn2)


def _attn_kernel(*refs, n_src):
    q_ref = refs[0]
    k_refs = refs[1:1 + n_src]
    v_refs = refs[1 + n_src:1 + 2 * n_src]
    o_ref = refs[1 + 2 * n_src]
    heads = range(2)
    sls = [slice(j * HEAD_SLOT, (j + 1) * HEAD_SLOT) for j in heads]
    scores = [[lax.dot_general(q_ref[:, sl], kr[:, sl], (((1,), (1,)), ((), ())), preferred_element_type=F32)
               for kr in k_refs] for sl in sls]
    for j in heads:
        m = None
        for s in scores[j]:
            ms = jnp.max(s, axis=-1, keepdims=True)
            m = ms if m is None else jnp.maximum(m, ms)
        acc = None
        for s, vr in zip(scores[j], v_refs):
            p = jnp.exp(s - m).astype(BF16)
            t = _mm(p, vr[:, sls[j]])
            acc = t if acc is None else acc + t
        out = acc[:, 0:V_DIM] / acc[:, V_DIM:V_DIM + 1]
        o_ref[:, j * V_DIM:(j + 1) * V_DIM] = out.astype(o_ref.dtype)


def _attention(q, ks, vs, batch, tq_total, tq, src_lens):
    n_src = len(ks)
    nq = tq_total // tq
    npair = MLA_HEADS // 2
    qmap = lambda b, p, i: (b * nq + i, p)
    kvmap = lambda b, p, i: (b, p)
    in_specs = [pl.BlockSpec((tq, 2 * HEAD_SLOT), qmap)]
    in_specs += [pl.BlockSpec((t, 2 * HEAD_SLOT), kvmap) for t in src_lens]
    in_specs += [pl.BlockSpec((t, 2 * HEAD_SLOT), kvmap) for t in src_lens]
    return pl.pallas_call(
        functools.partial(_attn_kernel, n_src=n_src),
        grid=(batch, npair, nq),
        in_specs=in_specs,
        out_specs=pl.BlockSpec((tq, 2 * V_DIM), qmap),
        out_shape=jax.ShapeDtypeStruct((batch * tq_total, MLA_HEADS * V_DIM), BF16),
        compiler_params=pltpu.CompilerParams(dimension_semantics=("arbitrary", "arbitrary", "arbitrary"),
                                             vmem_limit_bytes=VMEM_LIMIT),
        name="attention",
    )(q, *ks, *vs)


def _out_ffn_kernel(x_ref, yg_ref, ym_ref, wo_ref, g1_ref, sh2_ref, sc2_ref, g2_ref, n2_ref,
                    wg_ref, wu_ref, wd_ref, fn_ref, o_ref, *, final_norm, th):
    half = GDN_VAL_W
    y = _mm(yg_ref[...], wo_ref[0:half, :]) + _mm(ym_ref[...], wo_ref[half:, :])
    x1 = x_ref[...] + g1_ref[...] * y
    h2 = (_rms(x1, n2_ref[...]) * (1.0 + sc2_ref[...]) + sh2_ref[...]).astype(BF16)
    acc = None
    for j in range(FFN_HIDDEN // th):
        hs = slice(j * th, (j + 1) * th)
        act = (_silu(_mm(h2, wg_ref[:, hs])) * _mm(h2, wu_ref[:, hs])).astype(BF16)
        part = _mm(act, wd_ref[hs, :])
        acc = part if acc is None else acc + part
    x2 = x1 + g2_ref[...] * acc
    if final_norm:
        x2 = _rms(x2, fn_ref[...])
    o_ref[...] = x2


def _out_ffn(x, yg, ym, wo, mod, mod_row, n2, wg, wu, wd, fn, seq, tm, th, final_norm, layer):
    n, d = x.shape
    per_b = seq // tm
    const = lambda i: (0, 0)
    row = lambda i: (i, 0)
    hid = FFN_HIDDEN
    mod_specs = [_mod_spec(d, layer, part, per_b, mod_row) for part in (2, 3, 4, 5)]
    resident = lambda shape: pl.BlockSpec((None,) + shape, lambda i: (layer, 0, 0),
                                          pipeline_mode=pl.Buffered(1))
    return pl.pallas_call(
        functools.partial(_out_ffn_kernel, final_norm=final_norm, th=th),
        grid=(n // tm,),
        in_specs=[pl.BlockSpec((tm, d), row),
                  pl.BlockSpec((tm, GDN_VAL_W), row), pl.BlockSpec((tm, MLA_HEADS * V_DIM), row),
                  resident((d, d)),
                  *mod_specs,
                  pl.BlockSpec((1, d), const),
                  resident((d, hid)), resident((d, hid)), resident((hid, d)),
                  pl.BlockSpec((1, d), const)],
        out_specs=pl.BlockSpec((tm, d), row),
        out_shape=jax.ShapeDtypeStruct((n, d), F32),
        compiler_params=pltpu.CompilerParams(dimension_semantics=("arbitrary",),
                                             vmem_limit_bytes=VMEM_LIMIT),
        name="out_ffn",
    )(x, yg, ym, wo, mod, mod, mod, mod, n2, wg, wu, wd, fn)


_ROT_PERM = np.concatenate([np.arange(8, 16), np.arange(0, 8), np.arange(24, 32), np.arange(16, 24)])
_ROT_SIGN = np.concatenate([-np.ones(8), np.ones(8), -np.ones(8), np.ones(8)]).astype(np.float32)


def _rot_cols(w):
    return w[..., _ROT_PERM] * _ROT_SIGN


def _arrange_w_in(w_in):
    o = np.cumsum([0, CONV_W, GDN_VAL_W, 2 * GDN_HEADS, 2 * GDN_HEADS, Q_LORA, KV_LORA, QK_ROPE])
    qkv, z, a, b, cq, ckv, kr = (w_in[..., o[i]:o[i + 1]] for i in range(7))
    pad = jnp.zeros(w_in.shape[:-1] + (LANES - 4 * GDN_HEADS - 2 * QK_ROPE,), w_in.dtype)
    return jnp.concatenate([qkv, z, cq, ckv, a, b, pad, kr, _rot_cols(kr)], axis=-1).astype(BF16)


def _arrange_w_uq(w_uq):
    lead = w_uq.shape[:-1]
    w = w_uq.reshape(lead + (MLA_HEADS, QK_NOPE + QK_ROPE))
    pe = w[..., QK_NOPE:]
    return jnp.concatenate([w[..., :QK_NOPE], pe, _rot_cols(pe)], axis=-1).reshape(lead + (ATT_W,)).astype(BF16)


def _arrange_w_ukv(w_ukv):
    lead = w_ukv.shape[:-1]
    w = w_ukv.reshape(lead + (MLA_HEADS, QK_NOPE + V_DIM))
    zk = jnp.zeros(lead + (MLA_HEADS, HEAD_SLOT - QK_NOPE), w.dtype)
    zv = jnp.zeros(lead + (MLA_HEADS, HEAD_SLOT - V_DIM), w.dtype)
    wk = jnp.concatenate([w[..., :QK_NOPE], zk], axis=-1).reshape(lead + (ATT_W,))
    wv = jnp.concatenate([w[..., QK_NOPE:], zv], axis=-1).reshape(lead + (ATT_W,))
    return jnp.concatenate([wk, wv], axis=-1).astype(BF16)


def _rope_tables(t_lat, t_ctx):
    rows = t_lat // GRID_W
    row = np.repeat(np.arange(rows), GRID_W).astype(np.float64)
    col = np.tile(np.arange(GRID_W), rows).astype(np.float64)
    inv_freq = ROPE_THETA ** (-np.arange(0, AXIS_DIM, 2, dtype=np.float64) / AXIS_DIM)

    def axis_angles(pos):
        ang = pos[:, None] * inv_freq[None, :]
        return np.concatenate([ang, ang], axis=-1)

    ang = np.concatenate([axis_angles(row), axis_angles(col)], axis=-1)
    scale = (QK_NOPE + QK_ROPE) ** -0.5

    def build(cos_t, sin_t):
        t = cos_t.shape[0]
        z32 = np.zeros((t, QK_ROPE))
        z64 = np.zeros((t, QK_NOPE))
        cosq = np.concatenate([np.ones((t, QK_NOPE)), cos_t, z32], axis=1) * scale
        sinq = np.concatenate([z64, sin_t, z32], axis=1) * scale
        cosk = np.concatenate([z64, cos_t, z32], axis=1)
        sink = np.concatenate([z64, sin_t, z32], axis=1)
        return jnp.asarray(np.concatenate([cosq, sinq, cosk, sink], axis=1), F32)

    lat = build(np.cos(ang), np.sin(ang))
    ctx = build(np.ones((t_ctx, QK_ROPE)), np.zeros((t_ctx, QK_ROPE)))
    return lat, ctx


def kernel(x, c, ctx, c_ctx, w_mod, b_mod, norm1_g, norm2_g, w_in, conv_w, a_log, dt_bias, gdn_norm_g,
           q_norm_g, w_uq, kv_norm_g, w_ukv, w_out, w_gate, w_up, w_down, final_norm_g):
    batch, seq, d = x.shape
    tctx = ctx.shape[1]
    tab_l, tab_c = _rope_tables(seq, tctx)
    vone_np = np.zeros((MLA_HEADS, HEAD_SLOT), np.float32)
    vone_np[:, V_DIM] = 1.0
    vone = jnp.asarray(vone_np.reshape(1, ATT_W))
    mod_rows = 2 * SUBLANES
    ctx_row = batch
    cc = jnp.concatenate([c, c_ctx[None, :], jnp.zeros((mod_rows - batch - 1, d), F32)], axis=0)
    xs = x.reshape(batch * seq, d)
    cs = ctx.reshape(batch * tctx, d)
    assert seq % TM_FFN == 0 and seq % TM_IN == 0 and tctx % TM_IN == 0 and (batch * tctx) % TM_FFN == 0
    assert seq % GRID_W == 0 and seq % CHUNK == 0 and tctx % CHUNK == 0 and batch < mod_rows

    mod = _modulation(cc, w_mod, b_mod[:, None, :]).reshape(DEPTH, mod_rows, 1, 6 * d)
    win, wq, wkv = _arrange_w_in(w_in), _arrange_w_uq(w_uq), _arrange_w_ukv(w_ukv)
    wo, wg, wu, wd = w_out.astype(BF16), w_gate.astype(BF16), w_up.astype(BF16), w_down.astype(BF16)
    gate_pad = ((0, 0), (0, LANES - 2 * GDN_HEADS))
    alog_all = jnp.pad(a_log.reshape(DEPTH, -1), gate_pad)
    dtb_all = jnp.pad(dt_bias.reshape(DEPTH, -1), gate_pad)
    gn2_all = jnp.tile(gdn_norm_g, (1, 2))
    fn = final_norm_g[None, :]

    for l in range(DEPTH):
        last = l == DEPTH - 1
        n1, n2 = norm1_g[l:l + 1], norm2_g[l:l + 1]
        qg, kvg = q_norm_g[l:l + 1], kv_norm_g[l:l + 1]
        alog, dtb = alog_all[l:l + 1], dtb_all[l:l + 1]
        qkv_c, z_c, gate_c, q_c, k_c, v_c = _in_proj(cs, mod, ctx_row, n1, win, qg, wq, kvg, wkv, vone,
                                                     tab_c, alog, dtb, tctx, TM_IN, l)
        qkv_l, z_l, gate_l, q_l, k_l, v_l = _in_proj(xs, mod, None, n1, win, qg, wq, kvg, wkv, vone,
                                                     tab_l, alog, dtb, seq, TM_IN, l)
        yg_c, yg_l = _gdn(qkv_c, gate_c, z_c, qkv_l, gate_l, z_l, conv_w[l], gn2_all[l:l + 1],
                          batch, tctx, seq)
        ym_l = _attention(q_l, (k_c, k_l), (v_c, v_l), batch, seq, min(seq, TQ_ATT), (tctx, seq))
        xs = _out_ffn(xs, yg_l, ym_l, wo, mod, None, n2, wg, wu, wd, fn, seq, TM_FFN, FFN_TILE, last, l)
        if not last:
            ym_c = _attention(q_c, (k_c,), (v_c,), batch, tctx, tctx, (tctx,))
            cs = _out_ffn(cs, yg_c, ym_c, wo, mod, ctx_row, n2, wg, wu, wd, fn, tctx, TM_FFN, FFN_TILE, False, l)
    return xs.reshape(batch, seq, d)
```

```python
import functools
import math

import jax
import jax.numpy as jnp
import numpy as np
from jax import lax
from jax.experimental import pallas as pl
from jax.experimental.pallas import tpu as pltpu

D_MODEL = 1024
DEPTH = 2
GRID_W = 64
EPS = 1e-6

GDN_HEADS = 8
GDN_DK = 64
GDN_DV = 64
GDN_KEY_W = GDN_HEADS * GDN_DK
GDN_VAL_W = GDN_HEADS * GDN_DV
CONV_W = 2 * GDN_KEY_W + GDN_VAL_W
CONV_K = 5
CHUNK = 64
GDN_SUB = 8

MLA_HEADS = 8
QK_NOPE = 64
QK_ROPE = 32
V_DIM = 64
Q_LORA = 256
KV_LORA = 128
ROPE_THETA = 10000.0
AXIS_DIM = QK_ROPE // 2
FFN_HIDDEN = -(-8 * D_MODEL // (3 * 256)) * 256

LANES = 128
SUBLANES = 8
MXU_WIDTH = 256
HEAD_SLOT = LANES
ATT_W = MLA_HEADS * HEAD_SLOT
IN_W = CONV_W + GDN_VAL_W + Q_LORA + KV_LORA + LANES
VMEM_LIMIT = 56 * 1024 * 1024

TM_IN = 256
TM_FFN = 512
TQ_ATT = 1024
FFN_TILE = MXU_WIDTH

F32 = jnp.float32
BF16 = jnp.bfloat16


def _sigmoid(x):
    return 0.5 * (1.0 + jnp.tanh(0.5 * x))


def _silu(x):
    return x * _sigmoid(x)


def _rms(x, g):
    return x * lax.rsqrt(jnp.mean(x * x, axis=-1, keepdims=True) + EPS) * g


def _mm(a, b):
    return jnp.dot(a, b, preferred_element_type=F32)


def _mod_kernel(c_ref, w_ref, b_ref, o_ref):
    o_ref[...] = _mm(_silu(c_ref[...]).astype(BF16), w_ref[...].astype(BF16)) + b_ref[...]


def _modulation(cc, w_mod, b_mod):
    rows, d = cc.shape
    depth, _, n = w_mod.shape
    tn = 1024
    return pl.pallas_call(
        _mod_kernel,
        grid=(depth, n // tn),
        in_specs=[pl.BlockSpec((rows, d), lambda l, j: (0, 0)),
                  pl.BlockSpec((None, d, tn), lambda l, j: (l, 0, j)),
                  pl.BlockSpec((None, 1, tn), lambda l, j: (l, 0, j))],
        out_specs=pl.BlockSpec((None, rows, tn), lambda l, j: (l, 0, j)),
        out_shape=jax.ShapeDtypeStruct((depth, rows, n), F32),
        compiler_params=pltpu.CompilerParams(dimension_semantics=("arbitrary", "arbitrary"),
                                             vmem_limit_bytes=VMEM_LIMIT),
        name="modulation",
    )(cc, w_mod, b_mod)


def _split3(x, axis):
    hi = x.astype(BF16).astype(F32)
    mid = (x - hi).astype(BF16).astype(F32)
    return jnp.concatenate([hi, mid, x - hi - mid], axis=axis).astype(BF16)


def _in_kernel(x_ref, sh_ref, sc_ref, g_ref, win_ref, qg_ref, wq_ref, kvg_ref, wkv_ref, vone_ref, tab_ref,
               alog_ref, dtb_ref, cump_ref, cums_ref,
               qkv_ref, z_ref, gate_ref, q_ref, k_ref, v_ref):
    x = x_ref[...]
    h = _rms(x, g_ref[...]) * (1.0 + sc_ref[...]) + sh_ref[...]
    p = _mm(h.astype(BF16), win_ref[...])
    o1 = CONV_W
    o2 = o1 + GDN_VAL_W
    o3 = o2 + Q_LORA
    o4 = o3 + KV_LORA
    qkv_ref[...] = p[:, :o1]
    z_ref[...] = p[:, o1:o2]
    misc = p[:, o4:]
    lane = lax.broadcasted_iota(jnp.int32, (1, LANES), 1)
    ngate = 2 * GDN_HEADS
    a = misc + dtb_ref[...]
    gdec = -jnp.exp(alog_ref[...]) * (jnp.maximum(a, 0.0) + jnp.log(1.0 + jnp.exp(-jnp.abs(a))))
    g3 = _split3(jnp.where(lane < ngate, gdec, 0.0), 0)
    gsum = jnp.where(lane < GDN_HEADS, _mm(cump_ref[...], g3), _mm(cums_ref[...], g3))
    gate_ref[...] = jnp.where(lane < ngate, gsum, jnp.where(lane < 2 * ngate, _sigmoid(misc), 0.0))
    cqn = _rms(p[:, o2:o3], qg_ref[...]).astype(BF16)
    ckvn = _rms(p[:, o3:o4], kvg_ref[...]).astype(BF16)
    qa = _mm(cqn, wq_ref[...])
    kv = _mm(ckvn, wkv_ref[...])
    tab = tab_ref[...]
    cosq, sinq = tab[:, 0:LANES], tab[:, LANES:2 * LANES]
    cosk, sink = tab[:, 2 * LANES:3 * LANES], tab[:, 3 * LANES:4 * LANES]
    kpe = misc * cosk + pltpu.roll(misc, LANES - QK_ROPE, 1) * sink
    for hd in range(MLA_HEADS):
        sl = slice(hd * HEAD_SLOT, (hd + 1) * HEAD_SLOT)
        qh = qa[:, sl]
        q_ref[:, sl] = (qh * cosq + pltpu.roll(qh, LANES - QK_ROPE, 1) * sinq).astype(BF16)
        k_ref[:, sl] = (kv[:, sl] + kpe).astype(BF16)
    v_ref[...] = (kv[:, ATT_W:] + vone_ref[...]).astype(BF16)


def _chunk_cumsum_mats(tm):
    i = np.arange(tm)
    same = (i[:, None] // CHUNK) == (i[None, :] // CHUNK)
    pre = (same & (i[None, :] <= i[:, None])).astype(np.float32)
    suf = (same & (i[None, :] >= i[:, None])).astype(np.float32)
    return jnp.asarray(np.tile(pre, (1, 3)), BF16), jnp.asarray(np.tile(suf, (1, 3)), BF16)


def _mod_spec(d, layer, part, per_b, fixed_row):
    if fixed_row is None:
        return pl.BlockSpec((None, None, 1, d), lambda i: (layer, i // per_b, 0, part))
    return pl.BlockSpec((None, None, 1, d), lambda i: (layer, fixed_row, 0, part))


def _in_proj(x, mod, mod_row, g, win, qg, wq, kvg, wkv, vone, tab, alog, dtb, seq, tm, layer):
    n, d = x.shape
    cump, cums = _chunk_cumsum_mats(tm)
    per_b = seq // tm
    const = lambda i: (0, 0)
    row = lambda i: (i, 0)
    weight = lambda a, b: pl.BlockSpec((None, a, b), lambda i: (layer, 0, 0), pipeline_mode=pl.Buffered(1))
    out_shapes = (jax.ShapeDtypeStruct((n, CONV_W), F32), jax.ShapeDtypeStruct((n, GDN_VAL_W), F32),
                  jax.ShapeDtypeStruct((n, LANES), F32), jax.ShapeDtypeStruct((n, ATT_W), BF16),
                  jax.ShapeDtypeStruct((n, ATT_W), BF16), jax.ShapeDtypeStruct((n, ATT_W), BF16))
    return pl.pallas_call(
        _in_kernel,
        grid=(n // tm,),
        in_specs=[pl.BlockSpec((tm, d), row),
                  _mod_spec(d, layer, 0, per_b, mod_row), _mod_spec(d, layer, 1, per_b, mod_row),
                  pl.BlockSpec((1, d), const), weight(d, IN_W),
                  pl.BlockSpec((1, Q_LORA), const), weight(Q_LORA, ATT_W),
                  pl.BlockSpec((1, KV_LORA), const), weight(KV_LORA, 2 * ATT_W),
                  pl.BlockSpec((1, ATT_W), const),
                  pl.BlockSpec((tm, 4 * LANES), lambda i: (i % per_b, 0)),
                  pl.BlockSpec((1, LANES), const), pl.BlockSpec((1, LANES), const),
                  pl.BlockSpec((tm, 3 * tm), const), pl.BlockSpec((tm, 3 * tm), const)],
        out_specs=[pl.BlockSpec((tm, CONV_W), row), pl.BlockSpec((tm, GDN_VAL_W), row),
                   pl.BlockSpec((tm, LANES), row), pl.BlockSpec((tm, ATT_W), row),
                   pl.BlockSpec((tm, ATT_W), row), pl.BlockSpec((tm, ATT_W), row)],
        out_shape=out_shapes,
        compiler_params=pltpu.CompilerParams(dimension_semantics=("arbitrary",),
                                             vmem_limit_bytes=VMEM_LIMIT),
        name="in_proj",
    )(x, mod, mod, g, win, qg, wq, kvg, wkv, vone, tab, alog, dtb, cump, cums)


def _conv_silu(x_ref, w_ref, pad_s):
    t = x_ref.shape[0]
    zero = jnp.zeros((SUBLANES, LANES), F32)
    pad_s[0:SUBLANES, :] = zero
    pad_s[SUBLANES + t:2 * SUBLANES + t, :] = zero
    pad_s[SUBLANES:SUBLANES + t, :] = x_ref[...]
    acc = None
    for j in range(CONV_K):
        off = SUBLANES - CONV_K // 2 + j
        term = pad_s[off:off + t, :] * w_ref[j:j + 1, :]
        acc = term if acc is None else acc + term
    return _silu(acc)


def _hi_lo(x):
    hi = x.astype(BF16).astype(F32)
    return hi, (x - hi).astype(BF16).astype(F32)


def _seg_sum(x, gmat2):
    hi, lo = _hi_lo(x)
    return _mm(jnp.concatenate([hi, lo], axis=1).astype(BF16), gmat2)


def _bd_rows(x, left):
    return jnp.concatenate([jnp.where(left, x, 0.0), jnp.where(left, 0.0, x)], axis=0)


def _bd_w(x, left):
    return _bd_rows(x, left).astype(BF16)


def _tri_inverse_pair(amats, lvl_mask, eye, left):
    c = amats[0].shape[0]
    n1 = [jnp.where(lvl_mask[0], -a, 0.0) for a in amats]
    n2 = [_mm(n.astype(BF16), _bd_w(n, left)) for n in n1]
    t = [eye + n for n in n1]
    y = [_mm(jnp.concatenate([x, n], axis=0).astype(BF16), _bd_w(n, left)) for x, n in zip(t, n2)]
    t = [x + yy[0:c] for x, yy in zip(t, y)]
    t = [x + _mm(x.astype(BF16), _bd_w(yy[c:2 * c], left)) for x, yy in zip(t, y)]
    for m in lvl_mask[1:]:
        tq = [_mm(x.astype(BF16), _bd_w(jnp.where(m, a, 0.0), left)) for x, a in zip(t, amats)]
        t = [x - _mm(y_.astype(BF16), _bd_w(x, left)) for x, y_ in zip(t, tq)]
    return t


def _gdn_kernel(qc_ref, kc_ref, vc_ref, gc_ref, zc_ref, ql_ref, kl_ref, vl_ref, gl_ref, zl_ref,
                wq_ref, wk_ref, wv_ref, gn_ref,
                yc_ref, yl_ref,
                q_s, k_s, v_s, gcc_s, bb_s, acc_s, pad_s):
    pair = pl.program_id(1)
    c = CHUNK
    tc = qc_ref.shape[0]
    tl = ql_ref.shape[0]
    ngate = 2 * GDN_HEADS
    ri = lax.broadcasted_iota(jnp.int32, (c, LANES), 0)
    li = lax.broadcasted_iota(jnp.int32, (c, LANES), 1)
    ci = li & (c - 1)
    left = li < c
    incl = (ri >= ci, ri <= ci)
    strict = (ri > ci, ri < ci)
    eye = (ri == ci).astype(F32)
    bi, bj = ri >> 3, ci >> 3
    lvl_mask = (bi == bj,
                ((bi >> 1) == (bj >> 1)) & (bi != bj),
                ((bi >> 2) == (bj >> 2)) & ((bi >> 1) != (bj >> 1)),
                (bi >> 2) != (bj >> 2))
    rr = lax.broadcasted_iota(jnp.int32, (LANES, LANES), 0)
    ll = lax.broadcasted_iota(jnp.int32, (LANES, LANES), 1)
    bd = (rr >> 6) == (ll >> 6)
    top = rr < c
    gmat = bd.astype(BF16)
    gmat2 = jnp.concatenate([gmat, gmat], axis=0)
    src = lax.broadcasted_iota(jnp.int32, (3 * LANES, 4 * LANES), 0) & (LANES - 1)
    col = lax.broadcasted_iota(jnp.int32, (3 * LANES, 4 * LANES), 1)
    quantity, hh = col >> 7, (col >> 6) & 1
    want = (quantity & 1) * GDN_HEADS + (quantity >> 1) * ngate + 2 * pair + hh
    expand = (src == want).astype(BF16)

    def stage(q_ref, k_ref, v_ref, g_ref, base, t):
        uq = _conv_silu(q_ref, wq_ref, pad_s)
        q_s[base:base + t, :] = uq * lax.rsqrt(_seg_sum(uq * uq, gmat2) + EPS) * (GDN_DK ** -0.5)
        uk = _conv_silu(k_ref, wk_ref, pad_s)
        k_s[base:base + t, :] = uk * lax.rsqrt(_seg_sum(uk * uk, gmat2) + EPS)
        v_s[base:base + t, :] = _conv_silu(v_ref, wv_ref, pad_s)
        out = _mm(_split3(g_ref[...], 1), expand)
        for d in range(2):
            gcc_s[d, base:base + t, :] = out[:, d * LANES:(d + 1) * LANES]
            bb_s[d, base:base + t, :] = out[:, (2 + d) * LANES:(3 + d) * LANES]

    acc_s[...] = jnp.zeros_like(acc_s)
    stage(qc_ref, kc_ref, vc_ref, gc_ref, 0, tc)
    stage(ql_ref, kl_ref, vl_ref, gl_ref, tc, tl)

    def chunk_step(streams, starts, states):
        n = range(len(streams))
        dirs = [d for d, u in streams]
        rows = [pl.ds(s, c) for s in starts]
        q = [q_s[r, :] for r in rows]
        k = [k_s[r, :] for r in rows]
        v = [v_s[r, :] for r in rows]
        gcc = [gcc_s[d, r, :] for d, r in zip(dirs, rows)]
        bb = [bb_s[d, r, :] for d, r in zip(dirs, rows)]
        grow = [jnp.sum(eye * g, axis=0, keepdims=True) for g in gcc]
        decay = [jnp.where(incl[d], jnp.exp(jnp.where(incl[d], gcc[i] - grow[i], 0.0)), 0.0)
                 for i, d in enumerate(dirs)]
        kb = [k[i] * bb[i] for i in n]
        kt = [jnp.transpose(k[i]) for i in n]
        kbd = [jnp.where(bd, jnp.concatenate([kt[i], kt[i]], axis=1), 0.0).astype(BF16) for i in n]
        pm = [_mm(jnp.concatenate([kb[i], q[i]], axis=0).astype(BF16), kbd[i]) for i in n]
        amat = [jnp.where(strict[d], pm[i][0:c] * decay[i], 0.0) for i, d in enumerate(dirs)]
        qk = [(pm[i][c:2 * c] * decay[i]).astype(BF16) for i in n]
        tinv = _tri_inverse_pair(amat, lvl_mask, eye, left)
        egc = [jnp.exp(g) for g in gcc]
        gl = [gcc[i][c - 1:c, :] if d == 0 else gcc[i][0:1, :] for i, d in enumerate(dirs)]
        rhs = [jnp.concatenate([_bd_rows(kb[i] * egc[i], left), _bd_rows(v[i] * bb[i], left)], axis=1).astype(BF16)
               for i in n]
        wu = [_mm(tinv[i].astype(BF16), rhs[i]) for i in n]
        wub = [x.astype(BF16) for x in wu]
        wbd = [jnp.concatenate([_bd_rows(wu[i][:, 0:LANES], left), _bd_rows(wu[i][:, LANES:], left)],
                               axis=1).astype(BF16) for i in n]
        kscale = [jnp.exp(gl[i] - grow[i]) for i in n]
        kgt = [(kt[i] * jnp.where(top[:, 0:c], kscale[i][:, 0:c], kscale[i][:, c:])).astype(BF16) for i in n]
        qkwu = [_mm(qk[i], wbd[i]) for i in n]
        kgwu = [_mm(kgt[i], wub[i]) for i in n]
        lhs = [jnp.concatenate([jnp.where(bd, -kgwu[i][:, 0:LANES], 0.0),
                                q[i] * egc[i] - qkwu[i][:, 0:LANES]], axis=0).astype(BF16) for i in n]
        nmat = [jnp.where(bd, kgwu[i][:, LANES:], 0.0) for i in n]
        egl = [jnp.exp(g) for g in gl]
        st = list(states)
        for u in sorted({u for d, u in streams}):
            sel = [i for i in n if streams[i][1] == u]
            mqs = {i: _mm(lhs[i], st[dirs[i]].astype(BF16)) for i in sel}
            for i in sel:
                d = dirs[i]
                st[d] = st[d] * egl[i] + mqs[i][0:LANES] + nmat[i]
                acc_s[rows[i], :] = acc_s[rows[i], :] + qkwu[i][:, LANES:] + mqs[i][LANES:]
        return tuple(st)

    def run(base, nchunks, nsub, states):
        streams = [(d, u) for u in range(nsub) for d in range(2)]

        def body(i, st):
            pos = [i * nsub + u for d, u in streams]
            idx = [p if d == 0 else nchunks - 1 - p for p, (d, u) in zip(pos, streams)]
            return chunk_step(streams, [pl.multiple_of(base + x * c, c) for x in idx], st)
        return lax.fori_loop(0, nchunks // nsub, body, states)

    zero = jnp.zeros((LANES, LANES), F32)
    states = run(0, tc // c, math.gcd(tc // c, GDN_SUB), (zero, zero))
    run(tc, tl // c, math.gcd(tl // c, GDN_SUB), states)

    def finish(z_ref, y_ref, base, t):
        o = acc_s[base:base + t, :]
        ms = _seg_sum(o * o, gmat2) * (1.0 / GDN_DV)
        y_ref[...] = (o * lax.rsqrt(ms + EPS) * gn_ref[...] * _silu(z_ref[...])).astype(y_ref.dtype)

    finish(zc_ref, yc_ref, 0, tc)
    finish(zl_ref, yl_ref, tc, tl)


def _gdn(qkv_c, gate_c, z_c, qkv_l, gate_l, z_l, conv_w, gn2, batch, tc, tl):
    npair = GDN_HEADS // 2
    kq, kk, kvv = 0, npair, 2 * npair
    blk = lambda t, off: pl.BlockSpec((t, LANES), lambda b, p: (b, off + p))
    mblk = lambda t: pl.BlockSpec((t, LANES), lambda b, p: (b, 0))
    wblk = lambda off: pl.BlockSpec((CONV_K, LANES), lambda b, p: (0, off + p))
    const = lambda b, p: (0, 0)
    tt = tc + tl
    seq_buf = pltpu.VMEM((tt, LANES), F32)
    return pl.pallas_call(
        _gdn_kernel,
        grid=(batch, npair),
        in_specs=[blk(tc, kq), blk(tc, kk), blk(tc, kvv), mblk(tc), blk(tc, 0),
                  blk(tl, kq), blk(tl, kk), blk(tl, kvv), mblk(tl), blk(tl, 0),
                  wblk(kq), wblk(kk), wblk(kvv),
                  pl.BlockSpec((1, LANES), const)],
        out_specs=[blk(tc, 0), blk(tl, 0)],
        out_shape=(jax.ShapeDtypeStruct((batch * tc, GDN_VAL_W), BF16),
                   jax.ShapeDtypeStruct((batch * tl, GDN_VAL_W), BF16)),
        scratch_shapes=[seq_buf, seq_buf, seq_buf,
                        pltpu.VMEM((2, tt, LANES), F32), pltpu.VMEM((2, tt, LANES), F32), seq_buf,
                        pltpu.VMEM((max(tc, tl) + 2 * SUBLANES, LANES), F32)],
        compiler_params=pltpu.CompilerParams(dimension_semantics=("arbitrary", "arbitrary"),
                                             vmem_limit_bytes=VMEM_LIMIT),
        name="gdn",
    )(qkv_c, qkv_c, qkv_c, gate_c, z_c, qkv_l, qkv_l, qkv_l, gate_l, z_l,
      conv_w, conv_w, conv_w, gn2)


def _attn_kernel(*refs, n_src):
    q_ref = refs[0]
    k_refs = refs[1:1 + n_src]
    v_refs = refs[1 + n_src:1 + 2 * n_src]
    o_ref = refs[1 + 2 * n_src]
    heads = range(2)
    sls = [slice(j * HEAD_SLOT, (j + 1) * HEAD_SLOT) for j in heads]
    scores = [[lax.dot_general(q_ref[:, sl], kr[:, sl], (((1,), (1,)), ((), ())), preferred_element_type=F32)
               for kr in k_refs] for sl in sls]
    for j in heads:
        m = None
        for s in scores[j]:
            ms = jnp.max(s, axis=-1, keepdims=True)
            m = ms if m is None else jnp.maximum(m, ms)
        acc = None
        for s, vr in zip(scores[j], v_refs):
            p = jnp.exp(s - m).astype(BF16)
            t = _mm(p, vr[:, sls[j]])
            acc = t if acc is None else acc + t
        out = acc[:, 0:V_DIM] / acc[:, V_DIM:V_DIM + 1]
        o_ref[:, j * V_DIM:(j + 1) * V_DIM] = out.astype(o_ref.dtype)


def _attention(q, ks, vs, batch, tq_total, tq, src_lens):
    n_src = len(ks)
    nq = tq_total // tq
    npair = MLA_HEADS // 2
    qmap = lambda b, p, i: (b * nq + i, p)
    kvmap = lambda b, p, i: (b, p)
    in_specs = [pl.BlockSpec((tq, 2 * HEAD_SLOT), qmap)]
    in_specs += [pl.BlockSpec((t, 2 * HEAD_SLOT), kvmap) for t in src_lens]
    in_specs += [pl.BlockSpec((t, 2 * HEAD_SLOT), kvmap) for t in src_lens]
    return pl.pallas_call(
        functools.partial(_attn_kernel, n_src=n_src),
        grid=(batch, npair, nq),
        in_specs=in_specs,
        out_specs=pl.BlockSpec((tq, 2 * V_DIM), qmap),
        out_shape=jax.ShapeDtypeStruct((batch * tq_total, MLA_HEADS * V_DIM), BF16),
        compiler_params=pltpu.CompilerParams(dimension_semantics=("arbitrary", "arbitrary", "arbitrary"),
                                             vmem_limit_bytes=VMEM_LIMIT),
        name="attention",
    )(q, *ks, *vs)


def _out_ffn_kernel(x_ref, yg_ref, ym_ref, wo_ref, g1_ref, sh2_ref, sc2_ref, g2_ref, n2_ref,
                    wg_ref, wu_ref, wd_ref, fn_ref, o_ref, *, final_norm, th):
    half = GDN_VAL_W
    y = _mm(yg_ref[...], wo_ref[0:half, :]) + _mm(ym_ref[...], wo_ref[half:, :])
    x1 = x_ref[...] + g1_ref[...] * y
    h2 = (_rms(x1, n2_ref[...]) * (1.0 + sc2_ref[...]) + sh2_ref[...]).astype(BF16)
    acc = None
    for j in range(FFN_HIDDEN // th):
        hs = slice(j * th, (j + 1) * th)
        act = (_silu(_mm(h2, wg_ref[:, hs])) * _mm(h2, wu_ref[:, hs])).astype(BF16)
        part = _mm(act, wd_ref[hs, :])
        acc = part if acc is None else acc + part
    x2 = x1 + g2_ref[...] * acc
    if final_norm:
        x2 = _rms(x2, fn_ref[...])
    o_ref[...] = x2


def _out_ffn(x, yg, ym, wo, mod, mod_row, n2, wg, wu, wd, fn, seq, tm, th, final_norm, layer):
    n, d = x.shape
    per_b = seq // tm
    const = lambda i: (0, 0)
    row = lambda i: (i, 0)
    hid = FFN_HIDDEN
    mod_specs = [_mod_spec(d, layer, part, per_b, mod_row) for part in (2, 3, 4, 5)]
    resident = lambda shape: pl.BlockSpec((None,) + shape, lambda i: (layer, 0, 0),
                                          pipeline_mode=pl.Buffered(1))
    return pl.pallas_call(
        functools.partial(_out_ffn_kernel, final_norm=final_norm, th=th),
        grid=(n // tm,),
        in_specs=[pl.BlockSpec((tm, d), row),
                  pl.BlockSpec((tm, GDN_VAL_W), row), pl.BlockSpec((tm, MLA_HEADS * V_DIM), row),
                  resident((d, d)),
                  *mod_specs,
                  pl.BlockSpec((1, d), const),
                  resident((d, hid)), resident((d, hid)), resident((hid, d)),
                  pl.BlockSpec((1, d), const)],
        out_specs=pl.BlockSpec((tm, d), row),
        out_shape=jax.ShapeDtypeStruct((n, d), F32),
        compiler_params=pltpu.CompilerParams(dimension_semantics=("arbitrary",),
                                             vmem_limit_bytes=VMEM_LIMIT),
        name="out_ffn",
    )(x, yg, ym, wo, mod, mod, mod, mod, n2, wg, wu, wd, fn)


_ROT_PERM = np.concatenate([np.arange(8, 16), np.arange(0, 8), np.arange(24, 32), np.arange(16, 24)])
_ROT_SIGN = np.concatenate([-np.ones(8), np.ones(8), -np.ones(8), np.ones(8)]).astype(np.float32)


def _rot_cols(w):
    return w[..., _ROT_PERM] * _ROT_SIGN


def _arrange_w_in(w_in):
    o = np.cumsum([0, CONV_W, GDN_VAL_W, 2 * GDN_HEADS, 2 * GDN_HEADS, Q_LORA, KV_LORA, QK_ROPE])
    qkv, z, a, b, cq, ckv, kr = (w_in[..., o[i]:o[i + 1]] for i in range(7))
    pad = jnp.zeros(w_in.shape[:-1] + (LANES - 4 * GDN_HEADS - 2 * QK_ROPE,), w_in.dtype)
    return jnp.concatenate([qkv, z, cq, ckv, a, b, pad, kr, _rot_cols(kr)], axis=-1).astype(BF16)


def _arrange_w_uq(w_uq):
    lead = w_uq.shape[:-1]
    w = w_uq.reshape(lead + (MLA_HEADS, QK_NOPE + QK_ROPE))
    pe = w[..., QK_NOPE:]
    return jnp.concatenate([w[..., :QK_NOPE], pe, _rot_cols(pe)], axis=-1).reshape(lead + (ATT_W,)).astype(BF16)


def _arrange_w_ukv(w_ukv):
    lead = w_ukv.shape[:-1]
    w = w_ukv.reshape(lead + (MLA_HEADS, QK_NOPE + V_DIM))
    zk = jnp.zeros(lead + (MLA_HEADS, HEAD_SLOT - QK_NOPE), w.dtype)
    zv = jnp.zeros(lead + (MLA_HEADS, HEAD_SLOT - V_DIM), w.dtype)
    wk = jnp.concatenate([w[..., :QK_NOPE], zk], axis=-1).reshape(lead + (ATT_W,))
    wv = jnp.concatenate([w[..., QK_NOPE:], zv], axis=-1).reshape(lead + (ATT_W,))
    return jnp.concatenate([wk, wv], axis=-1).astype(BF16)


def _rope_tables(t_lat, t_ctx):
    rows = t_lat // GRID_W
    row = np.repeat(np.arange(rows), GRID_W).astype(np.float64)
    col = np.tile(np.arange(GRID_W), rows).astype(np.float64)
    inv_freq = ROPE_THETA ** (-np.arange(0, AXIS_DIM, 2, dtype=np.float64) / AXIS_DIM)

    def axis_angles(pos):
        ang = pos[:, None] * inv_freq[None, :]
        return np.concatenate([ang, ang], axis=-1)

    ang = np.concatenate([axis_angles(row), axis_angles(col)], axis=-1)
    scale = (QK_NOPE + QK_ROPE) ** -0.5

    def build(cos_t, sin_t):
        t = cos_t.shape[0]
        z32 = np.zeros((t, QK_ROPE))
        z64 = np.zeros((t, QK_NOPE))
        cosq = np.concatenate([np.ones((t, QK_NOPE)), cos_t, z32], axis=1) * scale
        sinq = np.concatenate([z64, sin_t, z32], axis=1) * scale
        cosk = np.concatenate([z64, cos_t, z32], axis=1)
        sink = np.concatenate([z64, sin_t, z32], axis=1)
        return jnp.asarray(np.concatenate([cosq, sinq, cosk, sink], axis=1), F32)

    lat = build(np.cos(ang), np.sin(ang))
    ctx = build(np.ones((t_ctx, QK_ROPE)), np.zeros((t_ctx, QK_ROPE)))
    return lat, ctx


def kernel(x, c, ctx, c_ctx, w_mod, b_mod, norm1_g, norm2_g, w_in, conv_w, a_log, dt_bias, gdn_norm_g,
           q_norm_g, w_uq, kv_norm_g, w_ukv, w_out, w_gate, w_up, w_down, final_norm_g):
    batch, seq, d = x.shape
    tctx = ctx.shape[1]
    tab_l, tab_c = _rope_tables(seq, tctx)
    vone_np = np.zeros((MLA_HEADS, HEAD_SLOT), np.float32)
    vone_np[:, V_DIM] = 1.0
    vone = jnp.asarray(vone_np.reshape(1, ATT_W))
    mod_rows = 2 * SUBLANES
    ctx_row = batch
    cc = jnp.concatenate([c, c_ctx[None, :], jnp.zeros((mod_rows - batch - 1, d), F32)], axis=0)
    xs = x.reshape(batch * seq, d)
    cs = ctx.reshape(batch * tctx, d)
    assert seq % TM_FFN == 0 and seq % TM_IN == 0 and tctx % TM_IN == 0 and (batch * tctx) % TM_FFN == 0
    assert seq % GRID_W == 0 and seq % CHUNK == 0 and tctx % CHUNK == 0 and batch < mod_rows

    mod = _modulation(cc, w_mod, b_mod[:, None, :]).reshape(DEPTH, mod_rows, 1, 6 * d)
    win, wq, wkv = _arrange_w_in(w_in), _arrange_w_uq(w_uq), _arrange_w_ukv(w_ukv)
    wo, wg, wu, wd = w_out.astype(BF16), w_gate.astype(BF16), w_up.astype(BF16), w_down.astype(BF16)
    gate_pad = ((0, 0), (0, LANES - 2 * GDN_HEADS))
    alog_all = jnp.pad(a_log.reshape(DEPTH, -1), gate_pad)
    dtb_all = jnp.pad(dt_bias.reshape(DEPTH, -1), gate_pad)
    gn2_all = jnp.tile(gdn_norm_g, (1, 2))
    fn = final_norm_g[None, :]

    for l in range(DEPTH):
        last = l == DEPTH - 1
        n1, n2 = norm1_g[l:l + 1], norm2_g[l:l + 1]
        qg, kvg = q_norm_g[l:l + 1], kv_norm_g[l:l + 1]
        alog, dtb = alog_all[l:l + 1], dtb_all[l:l + 1]
        qkv_c, z_c, gate_c, q_c, k_c, v_c = _in_proj(cs, mod, ctx_row, n1, win, qg, wq, kvg, wkv, vone,
                                                     tab_c, alog, dtb, tctx, TM_IN, l)
        qkv_l, z_l, gate_l, q_l, k_l, v_l = _in_proj(xs, mod, None, n1, win, qg, wq, kvg, wkv, vone,
                                                     tab_l, alog, dtb, seq, TM_IN, l)
        yg_c, yg_l = _gdn(qkv_c, gate_c, z_c, qkv_l, gate_l, z_l, conv_w[l], gn2_all[l:l + 1],
                          batch, tctx, seq)
        ym_l = _attention(q_l, (k_c, k_l), (v_c, v_l), batch, seq, min(seq, TQ_ATT), (tctx, seq))
        xs = _out_ffn(xs, yg_l, ym_l, wo, mod, None, n2, wg, wu, wd, fn, seq, TM_FFN, FFN_TILE, last, l)
        if not last:
            ym_c = _attention(q_c, (k_c,), (v_c,), batch, tctx, tctx, (tctx,))
            cs = _out_ffn(cs, yg_c, ym_c, wo, mod, ctx_row, n2, wg, wu, wd, fn, tctx, TM_FFN, FFN_TILE, False, l)
    return xs.reshape(batch, seq, d)
```

```python
import functools
import math

import jax
import jax.numpy as jnp
import numpy as np
from jax import lax
from jax.experimental import pallas as pl
from jax.experimental.pallas import tpu as pltpu

D_MODEL = 1024
DEPTH = 2
GRID_W = 64
EPS = 1e-6

GDN_HEADS = 8
GDN_DK = 64
GDN_DV = 64
GDN_KEY_W = GDN_HEADS * GDN_DK
GDN_VAL_W = GDN_HEADS * GDN_DV
CONV_W = 2 * GDN_KEY_W + GDN_VAL_W
CONV_K = 5
CHUNK = 64
GDN_SUB = 8

MLA_HEADS = 8
QK_NOPE = 64
QK_ROPE = 32
V_DIM = 64
Q_LORA = 256
KV_LORA = 128
ROPE_THETA = 10000.0
AXIS_DIM = QK_ROPE // 2
FFN_HIDDEN = -(-8 * D_MODEL // (3 * 256)) * 256

LANES = 128
SUBLANES = 8
MXU_WIDTH = 256
HEAD_SLOT = LANES
ATT_W = MLA_HEADS * HEAD_SLOT
IN_COLS = CONV_W + GDN_VAL_W + 4 * GDN_HEADS + Q_LORA + KV_LORA + QK_ROPE
IN_W = -(-IN_COLS // LANES) * LANES
VMEM_LIMIT = 56 * 1024 * 1024

TM_IN = 256
TM_FFN = 512
TQ_ATT = 1024
FFN_TILE = MXU_WIDTH
ATT_HEADS_PER_STEP = 8

F32 = jnp.float32
BF16 = jnp.bfloat16


def _sigmoid(x):
    return 0.5 * (1.0 + jnp.tanh(0.5 * x))


def _silu(x):
    return x * _sigmoid(x)


def _rms(x, g):
    return x * lax.rsqrt(jnp.mean(x * x, axis=-1, keepdims=True) + EPS) * g


def _mm(a, b):
    return jnp.dot(a, b, preferred_element_type=F32)


def _mod_kernel(c_ref, w_ref, b_ref, o_ref):
    o_ref[...] = _mm(_silu(c_ref[...]).astype(BF16), w_ref[...].astype(BF16)) + b_ref[...]


def _modulation(cc, w_mod, b_mod):
    rows, d = cc.shape
    depth, _, n = w_mod.shape
    tn = 1024
    return pl.pallas_call(
        _mod_kernel,
        grid=(depth, n // tn),
        in_specs=[pl.BlockSpec((rows, d), lambda l, j: (0, 0)),
                  pl.BlockSpec((None, d, tn), lambda l, j: (l, 0, j)),
                  pl.BlockSpec((None, 1, tn), lambda l, j: (l, 0, j))],
        out_specs=pl.BlockSpec((None, rows, tn), lambda l, j: (l, 0, j)),
        out_shape=jax.ShapeDtypeStruct((depth, rows, n), F32),
        compiler_params=pltpu.CompilerParams(dimension_semantics=("arbitrary", "arbitrary"),
                                             vmem_limit_bytes=VMEM_LIMIT),
        name="modulation",
    )(cc, w_mod, b_mod)


def _split3(x, axis):
    hi = x.astype(BF16).astype(F32)
    mid = (x - hi).astype(BF16).astype(F32)
    return jnp.concatenate([hi, mid, x - hi - mid], axis=axis).astype(BF16)


def _in_kernel(x_ref, sh_ref, sc_ref, g_ref, win_ref, qg_ref, wq_ref, kvg_ref, wkv_ref, vone_ref, tab_ref,
               alog_ref, dtb_ref, cump_ref, cums_ref,
               qkv_ref, z_ref, gate_ref, q_ref, k_ref, v_ref):
    x = x_ref[...]
    h = _rms(x, g_ref[...]) * (1.0 + sc_ref[...]) + sh_ref[...]
    p = _mm(h.astype(BF16), win_ref[...])
    tm = p.shape[0]
    ngate = 2 * GDN_HEADS
    o = np.cumsum([0, CONV_W, GDN_VAL_W, ngate, ngate, Q_LORA, KV_LORA, QK_ROPE])
    qkv_ref[...] = p[:, o[0]:o[1]]
    z_ref[...] = p[:, o[1]:o[2]]
    gtile = p[:, o[2]:o[2] + LANES]
    lane = lax.broadcasted_iota(jnp.int32, (1, LANES), 1)
    a = gtile + dtb_ref[...]
    gdec = -jnp.exp(alog_ref[...]) * (jnp.maximum(a, 0.0) + jnp.log(1.0 + jnp.exp(-jnp.abs(a))))
    g3 = _split3(jnp.where(lane < ngate, gdec, 0.0), 0)
    gsum = jnp.where(lane < GDN_HEADS, _mm(cump_ref[...], g3), _mm(cums_ref[...], g3))
    gate_ref[...] = jnp.where(lane < ngate, gsum, jnp.where(lane < 2 * ngate, _sigmoid(gtile), 0.0))
    cqn = _rms(p[:, o[4]:o[5]], qg_ref[...]).astype(BF16)
    ckvn = _rms(p[:, o[5]:o[6]], kvg_ref[...]).astype(BF16)
    qa = _mm(cqn, wq_ref[...])
    kv = _mm(ckvn, wkv_ref[...])
    tab = tab_ref[...]
    cosq, sinq = tab[:, 0:LANES], tab[:, LANES:2 * LANES]
    cosk, sink = tab[:, 2 * LANES:3 * LANES], tab[:, 3 * LANES:4 * LANES]
    kr = jnp.concatenate([jnp.zeros((tm, QK_NOPE), F32), p[:, o[6]:o[7]],
                          jnp.zeros((tm, HEAD_SLOT - QK_NOPE - QK_ROPE), F32)], axis=1)
    half = AXIS_DIM // 2
    kr_rot = jnp.where((lane & (AXIS_DIM - 1)) < half, -pltpu.roll(kr, LANES - half, 1), pltpu.roll(kr, half, 1))
    kpe = kr * cosk + kr_rot * sink
    for hd in range(MLA_HEADS):
        sl = slice(hd * HEAD_SLOT, (hd + 1) * HEAD_SLOT)
        qh = qa[:, sl]
        q_ref[:, sl] = (qh * cosq + pltpu.roll(qh, LANES - QK_ROPE, 1) * sinq).astype(BF16)
        k_ref[:, sl] = (kv[:, sl] + kpe).astype(BF16)
    v_ref[...] = (kv[:, ATT_W:] + vone_ref[...]).astype(BF16)


def _chunk_cumsum_mats(tm):
    i = np.arange(tm)
    same = (i[:, None] // CHUNK) == (i[None, :] // CHUNK)
    pre = (same & (i[None, :] <= i[:, None])).astype(np.float32)
    suf = (same & (i[None, :] >= i[:, None])).astype(np.float32)
    return jnp.asarray(np.tile(pre, (1, 3)), BF16), jnp.asarray(np.tile(suf, (1, 3)), BF16)


def _mod_spec(d, layer, part, per_b, fixed_row):
    if fixed_row is None:
        return pl.BlockSpec((None, None, 1, d), lambda i: (layer, i // per_b, 0, part))
    return pl.BlockSpec((None, None, 1, d), lambda i: (layer, fixed_row, 0, part))


def _in_proj(x, mod, mod_row, g, win, qg, wq, kvg, wkv, vone, tab, alog, dtb, seq, tm, layer):
    n, d = x.shape
    cump, cums = _chunk_cumsum_mats(tm)
    per_b = seq // tm
    const = lambda i: (0, 0)
    row = lambda i: (i, 0)
    weight = lambda a, b: pl.BlockSpec((None, a, b), lambda i: (layer, 0, 0), pipeline_mode=pl.Buffered(1))
    out_shapes = (jax.ShapeDtypeStruct((n, CONV_W), F32), jax.ShapeDtypeStruct((n, GDN_VAL_W), F32),
                  jax.ShapeDtypeStruct((n, LANES), F32), jax.ShapeDtypeStruct((n, ATT_W), BF16),
                  jax.ShapeDtypeStruct((n, ATT_W), BF16), jax.ShapeDtypeStruct((n, ATT_W), BF16))
    return pl.pallas_call(
        _in_kernel,
        grid=(n // tm,),
        in_specs=[pl.BlockSpec((tm, d), row),
                  _mod_spec(d, layer, 0, per_b, mod_row), _mod_spec(d, layer, 1, per_b, mod_row),
                  pl.BlockSpec((1, d), const), weight(d, IN_W),
                  pl.BlockSpec((1, Q_LORA), const), weight(Q_LORA, ATT_W),
                  pl.BlockSpec((1, KV_LORA), const), weight(KV_LORA, 2 * ATT_W),
                  pl.BlockSpec((1, ATT_W), const),
                  pl.BlockSpec((tm, 4 * LANES), lambda i: (i % per_b, 0)),
                  pl.BlockSpec((1, LANES), const), pl.BlockSpec((1, LANES), const),
                  pl.BlockSpec((tm, 3 * tm), const), pl.BlockSpec((tm, 3 * tm), const)],
        out_specs=[pl.BlockSpec((tm, CONV_W), row), pl.BlockSpec((tm, GDN_VAL_W), row),
                   pl.BlockSpec((tm, LANES), row), pl.BlockSpec((tm, ATT_W), row),
                   pl.BlockSpec((tm, ATT_W), row), pl.BlockSpec((tm, ATT_W), row)],
        out_shape=out_shapes,
        compiler_params=pltpu.CompilerParams(dimension_semantics=("arbitrary",),
                                             vmem_limit_bytes=VMEM_LIMIT),
        name="in_proj",
    )(x, mod, mod, g, win, qg, wq, kvg, wkv, vone, tab, alog, dtb, cump, cums)


def _conv_silu(x_ref, w_ref, pad_s):
    t = x_ref.shape[0]
    zero = jnp.zeros((SUBLANES, LANES), F32)
    pad_s[0:SUBLANES, :] = zero
    pad_s[SUBLANES + t:2 * SUBLANES + t, :] = zero
    pad_s[SUBLANES:SUBLANES + t, :] = x_ref[...]
    acc = None
    for j in range(CONV_K):
        off = SUBLANES - CONV_K // 2 + j
        term = pad_s[off:off + t, :] * w_ref[j:j + 1, :]
        acc = term if acc is None else acc + term
    return _silu(acc)


def _hi_lo(x):
    hi = x.astype(BF16).astype(F32)
    return hi, (x - hi).astype(BF16).astype(F32)


def _seg_sum(x, gmat2):
    hi, lo = _hi_lo(x)
    return _mm(jnp.concatenate([hi, lo], axis=1).astype(BF16), gmat2)


def _bd_rows(x, left):
    return jnp.concatenate([jnp.where(left, x, 0.0), jnp.where(left, 0.0, x)], axis=0)


def _bd_w(x, left):
    return _bd_rows(x, left).astype(BF16)


def _tri_inverse_pair(amats, lvl_mask, eye, left):
    c = amats[0].shape[0]
    n1 = [jnp.where(lvl_mask[0], -a, 0.0) for a in amats]
    n2 = [_mm(n.astype(BF16), _bd_w(n, left)) for n in n1]
    t = [eye + n for n in n1]
    y = [_mm(jnp.concatenate([x, n], axis=0).astype(BF16), _bd_w(n, left)) for x, n in zip(t, n2)]
    t = [x + yy[0:c] for x, yy in zip(t, y)]
    t = [x + _mm(x.astype(BF16), _bd_w(yy[c:2 * c], left)) for x, yy in zip(t, y)]
    for m in lvl_mask[1:]:
        tq = [_mm(x.astype(BF16), _bd_w(jnp.where(m, a, 0.0), left)) for x, a in zip(t, amats)]
        t = [x - _mm(y_.astype(BF16), _bd_w(x, left)) for x, y_ in zip(t, tq)]
    return t


def _gdn_kernel(qc_ref, kc_ref, vc_ref, gc_ref, zc_ref, ql_ref, kl_ref, vl_ref, gl_ref, zl_ref,
                wq_ref, wk_ref, wv_ref, gn_ref,
                yc_ref, yl_ref,
                q_s, k_s, v_s, gcc_s, bb_s, acc_s, pad_s):
    pair = pl.program_id(1)
    c = CHUNK
    tc = qc_ref.shape[0]
    tl = ql_ref.shape[0]
    ngate = 2 * GDN_HEADS
    ri = lax.broadcasted_iota(jnp.int32, (c, LANES), 0)
    li = lax.broadcasted_iota(jnp.int32, (c, LANES), 1)
    ci = li & (c - 1)
    left = li < c
    incl = (ri >= ci, ri <= ci)
    strict = (ri > ci, ri < ci)
    eye = (ri == ci).astype(F32)
    bi, bj = ri >> 3, ci >> 3
    lvl_mask = (bi == bj,
                ((bi >> 1) == (bj >> 1)) & (bi != bj),
                ((bi >> 2) == (bj >> 2)) & ((bi >> 1) != (bj >> 1)),
                (bi >> 2) != (bj >> 2))
    rr = lax.broadcasted_iota(jnp.int32, (LANES, LANES), 0)
    ll = lax.broadcasted_iota(jnp.int32, (LANES, LANES), 1)
    bd = (rr >> 6) == (ll >> 6)
    top = rr < c
    gmat = bd.astype(BF16)
    gmat2 = jnp.concatenate([gmat, gmat], axis=0)
    src = lax.broadcasted_iota(jnp.int32, (3 * LANES, 4 * LANES), 0) & (LANES - 1)
    col = lax.broadcasted_iota(jnp.int32, (3 * LANES, 4 * LANES), 1)
    quantity, hh = col >> 7, (col >> 6) & 1
    want = (quantity & 1) * GDN_HEADS + (quantity >> 1) * ngate + 2 * pair + hh
    expand = (src == want).astype(BF16)

    def stage(q_ref, k_ref, v_ref, g_ref, base, t):
        uq = _conv_silu(q_ref, wq_ref, pad_s)
        q_s[base:base + t, :] = uq * lax.rsqrt(_seg_sum(uq * uq, gmat2) + EPS) * (GDN_DK ** -0.5)
        uk = _conv_silu(k_ref, wk_ref, pad_s)
        k_s[base:base + t, :] = uk * lax.rsqrt(_seg_sum(uk * uk, gmat2) + EPS)
        v_s[base:base + t, :] = _conv_silu(v_ref, wv_ref, pad_s)
        out = _mm(_split3(g_ref[...], 1), expand)
        for d in range(2):
            gcc_s[d, base:base + t, :] = out[:, d * LANES:(d + 1) * LANES]
            bb_s[d, base:base + t, :] = out[:, (2 + d) * LANES:(3 + d) * LANES]

    acc_s[...] = jnp.zeros_like(acc_s)
    stage(qc_ref, kc_ref, vc_ref, gc_ref, 0, tc)
    stage(ql_ref, kl_ref, vl_ref, gl_ref, tc, tl)

    def chunk_step(streams, starts, states):
        n = range(len(streams))
        dirs = [d for d, u in streams]
        rows = [pl.ds(s, c) for s in starts]
        q = [q_s[r, :] for r in rows]
        k = [k_s[r, :] for r in rows]
        v = [v_s[r, :] for r in rows]
        gcc = [gcc_s[d, r, :] for d, r in zip(dirs, rows)]
        bb = [bb_s[d, r, :] for d, r in zip(dirs, rows)]
        grow = [jnp.sum(eye * g, axis=0, keepdims=True) for g in gcc]
        decay = [jnp.where(incl[d], jnp.exp(jnp.where(incl[d], gcc[i] - grow[i], 0.0)), 0.0)
                 for i, d in enumerate(dirs)]
        kb = [k[i] * bb[i] for i in n]
        kt = [jnp.transpose(k[i]) for i in n]
        kbd = [jnp.where(bd, jnp.concatenate([kt[i], kt[i]], axis=1), 0.0).astype(BF16) for i in n]
        pm = [_mm(jnp.concatenate([kb[i], q[i]], axis=0).astype(BF16), kbd[i]) for i in n]
        amat = [jnp.where(strict[d], pm[i][0:c] * decay[i], 0.0) for i, d in enumerate(dirs)]
        qk = [(pm[i][c:2 * c] * decay[i]).astype(BF16) for i in n]
        tinv = _tri_inverse_pair(amat, lvl_mask, eye, left)
        egc = [jnp.exp(g) for g in gcc]
        gl = [gcc[i][c - 1:c, :] if d == 0 else gcc[i][0:1, :] for i, d in enumerate(dirs)]
        rhs = [jnp.concatenate([_bd_rows(kb[i] * egc[i], left), _bd_rows(v[i] * bb[i], left)], axis=1).astype(BF16)
               for i in n]
        wu = [_mm(tinv[i].astype(BF16), rhs[i]) for i in n]
        wub = [x.astype(BF16) for x in wu]
        wbd = [jnp.concatenate([_bd_rows(wu[i][:, 0:LANES], left), _bd_rows(wu[i][:, LANES:], left)],
                               axis=1).astype(BF16) for i in n]
        kscale = [jnp.exp(gl[i] - grow[i]) for i in n]
        kgt = [(kt[i] * jnp.where(top[:, 0:c], kscale[i][:, 0:c], kscale[i][:, c:])).astype(BF16) for i in n]
        qkwu = [_mm(qk[i], wbd[i]) for i in n]
        kgwu = [_mm(kgt[i], wub[i]) for i in n]
        lhs = [jnp.concatenate([jnp.where(bd, -kgwu[i][:, 0:LANES], 0.0),
                                q[i] * egc[i] - qkwu[i][:, 0:LANES]], axis=0).astype(BF16) for i in n]
        nmat = [jnp.where(bd, kgwu[i][:, LANES:], 0.0) for i in n]
        egl = [jnp.exp(g) for g in gl]
        st = list(states)
        for u in sorted({u for d, u in streams}):
            sel = [i for i in n if streams[i][1] == u]
            mqs = {i: _mm(lhs[i], st[dirs[i]].astype(BF16)) for i in sel}
            for i in sel:
                d = dirs[i]
                st[d] = st[d] * egl[i] + mqs[i][0:LANES] + nmat[i]
                acc_s[rows[i], :] = acc_s[rows[i], :] + qkwu[i][:, LANES:] + mqs[i][LANES:]
        return tuple(st)

    def run(base, nchunks, nsub, states):
        streams = [(d, u) for u in range(nsub) for d in range(2)]

        def body(i, st):
            pos = [i * nsub + u for d, u in streams]
            idx = [p if d == 0 else nchunks - 1 - p for p, (d, u) in zip(pos, streams)]
            return chunk_step(streams, [pl.multiple_of(base + x * c, c) for x in idx], st)
        return lax.fori_loop(0, nchunks // nsub, body, states)

    zero = jnp.zeros((LANES, LANES), F32)
    states = run(0, tc // c, math.gcd(tc // c, GDN_SUB), (zero, zero))
    run(tc, tl // c, math.gcd(tl // c, GDN_SUB), states)

    def finish(z_ref, y_ref, base, t):
        o = acc_s[base:base + t, :]
        ms = _seg_sum(o * o, gmat2) * (1.0 / GDN_DV)
        y_ref[...] = (o * lax.rsqrt(ms + EPS) * gn_ref[...] * _silu(z_ref[...])).astype(y_ref.dtype)

    finish(zc_ref, yc_ref, 0, tc)
    finish(zl_ref, yl_ref, tc, tl)


def _gdn(qkv_c, gate_c, z_c, qkv_l, gate_l, z_l, conv_w, gn2, batch, tc, tl):
    npair = GDN_HEADS // 2
    kq, kk, kvv = 0, npair, 2 * npair
    blk = lambda t, off: pl.BlockSpec((t, LANES), lambda b, p: (b, off + p))
    mblk = lambda t: pl.BlockSpec((t, LANES), lambda b, p: (b, 0))
    wblk = lambda off: pl.BlockSpec((CONV_K, LANES), lambda b, p: (0, off + p))
    const = lambda b, p: (0, 0)
    tt = tc + tl
    seq_buf = pltpu.VMEM((tt, LANES), F32)
    return pl.pallas_call(
        _gdn_kernel,
        grid=(batch, npair),
        in_specs=[blk(tc, kq), blk(tc, kk), blk(tc, kvv), mblk(tc), blk(tc, 0),
                  blk(tl, kq), blk(tl, kk), blk(tl, kvv), mblk(tl), blk(tl, 0),
                  wblk(kq), wblk(kk), wblk(kvv),
                  pl.BlockSpec((1, LANES), const)],
        out_specs=[blk(tc, 0), blk(tl, 0)],
        out_shape=(jax.ShapeDtypeStruct((batch * tc, GDN_VAL_W), BF16),
                   jax.ShapeDtypeStruct((batch * tl, GDN_VAL_W), BF16)),
        scratch_shapes=[seq_buf, seq_buf, seq_buf,
                        pltpu.VMEM((2, tt, LANES), F32), pltpu.VMEM((2, tt, LANES), F32), seq_buf,
                        pltpu.VMEM((max(tc, tl) + 2 * SUBLANES, LANES), F32)],
        compiler_params=pltpu.CompilerParams(dimension_semantics=("arbitrary", "arbitrary"),
                                             vmem_limit_bytes=VMEM_LIMIT),
        name="gdn",
    )(qkv_c, qkv_c, qkv_c, gate_c, z_c, qkv_l, qkv_l, qkv_l, gate_l, z_l,
      conv_w, conv_w, conv_w, gn2)


def _attn_kernel(*refs, n_src, nh):
    q_ref = refs[0]
    k_refs = refs[1:1 + n_src]
    v_refs = refs[1 + n_src:1 + 2 * n_src]
    o_ref = refs[1 + 2 * n_src]
    sls = [slice(j * HEAD_SLOT, (j + 1) * HEAD_SLOT) for j in range(nh)]

    def scores(j):
        return [lax.dot_general(q_ref[:, sls[j]], kr[:, sls[j]], (((1,), (1,)), ((), ())),
                                preferred_element_type=F32) for kr in k_refs]

    nxt = scores(0)
    for j in range(nh):
        cur = nxt
        if j + 1 < nh:
            nxt = scores(j + 1)
        m = None
        for s in cur:
            ms = jnp.max(s, axis=-1, keepdims=True)
            m = ms if m is None else jnp.maximum(m, ms)
        acc = None
        for s, vr in zip(cur, v_refs):
            p = jnp.exp(s - m).astype(BF16)
            t = _mm(p, vr[:, sls[j]])
            acc = t if acc is None else acc + t
        out = acc[:, 0:V_DIM] / acc[:, V_DIM:V_DIM + 1]
        o_ref[:, j * V_DIM:(j + 1) * V_DIM] = out.astype(o_ref.dtype)


def _attention(q, ks, vs, batch, tq_total, tq, src_lens, nh=ATT_HEADS_PER_STEP):
    n_src = len(ks)
    nq = tq_total // tq
    ngroup = MLA_HEADS // nh
    qmap = lambda b, p, i: (b * nq + i, p)
    kvmap = lambda b, p, i: (b, p)
    in_specs = [pl.BlockSpec((tq, nh * HEAD_SLOT), qmap)]
    in_specs += [pl.BlockSpec((t, nh * HEAD_SLOT), kvmap) for t in src_lens]
    in_specs += [pl.BlockSpec((t, nh * HEAD_SLOT), kvmap) for t in src_lens]
    return pl.pallas_call(
        functools.partial(_attn_kernel, n_src=n_src, nh=nh),
        grid=(batch, ngroup, nq),
        in_specs=in_specs,
        out_specs=pl.BlockSpec((tq, nh * V_DIM), qmap),
        out_shape=jax.ShapeDtypeStruct((batch * tq_total, MLA_HEADS * V_DIM), BF16),
        compiler_params=pltpu.CompilerParams(dimension_semantics=("arbitrary", "arbitrary", "arbitrary"),
                                             vmem_limit_bytes=VMEM_LIMIT),
        name="attention",
    )(q, *ks, *vs)


def _out_ffn_kernel(x_ref, yg_ref, ym_ref, wo_ref, g1_ref, sh2_ref, sc2_ref, g2_ref, n2_ref,
                    wg_ref, wu_ref, wd_ref, fn_ref, o_ref, *, final_norm, th):
    half = GDN_VAL_W
    y = _mm(yg_ref[...], wo_ref[0:half, :]) + _mm(ym_ref[...], wo_ref[half:, :])
    x1 = x_ref[...] + g1_ref[...] * y
    h2 = (_rms(x1, n2_ref[...]) * (1.0 + sc2_ref[...]) + sh2_ref[...]).astype(BF16)
    acc = None
    for j in range(FFN_HIDDEN // th):
        hs = slice(j * th, (j + 1) * th)
        act = (_silu(_mm(h2, wg_ref[:, hs])) * _mm(h2, wu_ref[:, hs])).astype(BF16)
        part = _mm(act, wd_ref[hs, :])
        acc = part if acc is None else acc + part
    x2 = x1 + g2_ref[...] * acc
    if final_norm:
        x2 = _rms(x2, fn_ref[...])
    o_ref[...] = x2


def _out_ffn(x, yg, ym, wo, mod, mod_row, n2, wg, wu, wd, fn, seq, tm, th, final_norm, layer):
    n, d = x.shape
    per_b = seq // tm
    const = lambda i: (0, 0)
    row = lambda i: (i, 0)
    hid = FFN_HIDDEN
    mod_specs = [_mod_spec(d, layer, part, per_b, mod_row) for part in (2, 3, 4, 5)]
    resident = lambda shape: pl.BlockSpec((None,) + shape, lambda i: (layer, 0, 0),
                                          pipeline_mode=pl.Buffered(1))
    return pl.pallas_call(
        functools.partial(_out_ffn_kernel, final_norm=final_norm, th=th),
        grid=(n // tm,),
        in_specs=[pl.BlockSpec((tm, d), row),
                  pl.BlockSpec((tm, GDN_VAL_W), row), pl.BlockSpec((tm, MLA_HEADS * V_DIM), row),
                  resident((d, d)),
                  *mod_specs,
                  pl.BlockSpec((1, d), const),
                  resident((d, hid)), resident((d, hid)), resident((hid, d)),
                  pl.BlockSpec((1, d), const)],
        out_specs=pl.BlockSpec((tm, d), row),
        out_shape=jax.ShapeDtypeStruct((n, d), F32),
        compiler_params=pltpu.CompilerParams(dimension_semantics=("arbitrary",),
                                             vmem_limit_bytes=VMEM_LIMIT),
        name="out_ffn",
    )(x, yg, ym, wo, mod, mod, mod, mod, n2, wg, wu, wd, fn)


_ROT_PERM = np.concatenate([np.arange(8, 16), np.arange(0, 8), np.arange(24, 32), np.arange(16, 24)])
_ROT_SIGN = np.concatenate([-np.ones(8), np.ones(8), -np.ones(8), np.ones(8)]).astype(np.float32)


def _rot_cols(w):
    return w[..., _ROT_PERM] * _ROT_SIGN


def _arrange_w_in(w_in):
    return jnp.pad(w_in.astype(BF16), ((0, 0), (0, 0), (0, IN_W - w_in.shape[-1])))


def _arrange_w_uq(w_uq):
    lead = w_uq.shape[:-1]
    w = w_uq.reshape(lead + (MLA_HEADS, QK_NOPE + QK_ROPE))
    pe = w[..., QK_NOPE:]
    return jnp.concatenate([w[..., :QK_NOPE], pe, _rot_cols(pe)], axis=-1).reshape(lead + (ATT_W,)).astype(BF16)


def _arrange_w_ukv(w_ukv):
    lead = w_ukv.shape[:-1]
    w = w_ukv.reshape(lead + (MLA_HEADS, QK_NOPE + V_DIM))
    zk = jnp.zeros(lead + (MLA_HEADS, HEAD_SLOT - QK_NOPE), w.dtype)
    zv = jnp.zeros(lead + (MLA_HEADS, HEAD_SLOT - V_DIM), w.dtype)
    wk = jnp.concatenate([w[..., :QK_NOPE], zk], axis=-1).reshape(lead + (ATT_W,))
    wv = jnp.concatenate([w[..., QK_NOPE:], zv], axis=-1).reshape(lead + (ATT_W,))
    return jnp.concatenate([wk, wv], axis=-1).astype(BF16)


def _rope_tables(t_lat, t_ctx):
    rows = t_lat // GRID_W
    row = np.repeat(np.arange(rows), GRID_W).astype(np.float64)
    col = np.tile(np.arange(GRID_W), rows).astype(np.float64)
    inv_freq = ROPE_THETA ** (-np.arange(0, AXIS_DIM, 2, dtype=np.float64) / AXIS_DIM)

    def axis_angles(pos):
        ang = pos[:, None] * inv_freq[None, :]
        return np.concatenate([ang, ang], axis=-1)

    ang = np.concatenate([axis_angles(row), axis_angles(col)], axis=-1)
    scale = (QK_NOPE + QK_ROPE) ** -0.5

    def build(cos_t, sin_t):
        t = cos_t.shape[0]
        z32 = np.zeros((t, QK_ROPE))
        z64 = np.zeros((t, QK_NOPE))
        cosq = np.concatenate([np.ones((t, QK_NOPE)), cos_t, z32], axis=1) * scale
        sinq = np.concatenate([z64, sin_t, z32], axis=1) * scale
        cosk = np.concatenate([z64, cos_t, z32], axis=1)
        sink = np.concatenate([z64, sin_t, z32], axis=1)
        return jnp.asarray(np.concatenate([cosq, sinq, cosk, sink], axis=1), F32)

    lat = build(np.cos(ang), np.sin(ang))
    ctx = build(np.ones((t_ctx, QK_ROPE)), np.zeros((t_ctx, QK_ROPE)))
    return lat, ctx


def kernel(x, c, ctx, c_ctx, w_mod, b_mod, norm1_g, norm2_g, w_in, conv_w, a_log, dt_bias, gdn_norm_g,
           q_norm_g, w_uq, kv_norm_g, w_ukv, w_out, w_gate, w_up, w_down, final_norm_g):
    batch, seq, d = x.shape
    tctx = ctx.shape[1]
    tab_l, tab_c = _rope_tables(seq, tctx)
    vone_np = np.zeros((MLA_HEADS, HEAD_SLOT), np.float32)
    vone_np[:, V_DIM] = 1.0
    vone = jnp.asarray(vone_np.reshape(1, ATT_W))
    mod_rows = 2 * SUBLANES
    ctx_row = batch
    cc = jnp.concatenate([c, c_ctx[None, :], jnp.zeros((mod_rows - batch - 1, d), F32)], axis=0)
    xs = x.reshape(batch * seq, d)
    cs = ctx.reshape(batch * tctx, d)
    assert seq % TM_FFN == 0 and seq % TM_IN == 0 and tctx % TM_IN == 0 and (batch * tctx) % TM_FFN == 0
    assert seq % GRID_W == 0 and seq % CHUNK == 0 and tctx % CHUNK == 0 and batch < mod_rows

    mod = _modulation(cc, w_mod, b_mod[:, None, :]).reshape(DEPTH, mod_rows, 1, 6 * d)
    win, wq, wkv = _arrange_w_in(w_in), _arrange_w_uq(w_uq), _arrange_w_ukv(w_ukv)
    wo, wg, wu, wd = w_out.astype(BF16), w_gate.astype(BF16), w_up.astype(BF16), w_down.astype(BF16)
    gate_pad = ((0, 0), (0, LANES - 2 * GDN_HEADS))
    alog_all = jnp.pad(a_log.reshape(DEPTH, -1), gate_pad)
    dtb_all = jnp.pad(dt_bias.reshape(DEPTH, -1), gate_pad)
    gn2_all = jnp.tile(gdn_norm_g, (1, 2))
    fn = final_norm_g[None, :]

    for l in range(DEPTH):
        last = l == DEPTH - 1
        n1, n2 = norm1_g[l:l + 1], norm2_g[l:l + 1]
        qg, kvg = q_norm_g[l:l + 1], kv_norm_g[l:l + 1]
        alog, dtb = alog_all[l:l + 1], dtb_all[l:l + 1]
        qkv_c, z_c, gate_c, q_c, k_c, v_c = _in_proj(cs, mod, ctx_row, n1, win, qg, wq, kvg, wkv, vone,
                                                     tab_c, alog, dtb, tctx, TM_IN, l)
        qkv_l, z_l, gate_l, q_l, k_l, v_l = _in_proj(xs, mod, None, n1, win, qg, wq, kvg, wkv, vone,
                                                     tab_l, alog, dtb, seq, TM_IN, l)
        yg_c, yg_l = _gdn(qkv_c, gate_c, z_c, qkv_l, gate_l, z_l, conv_w[l], gn2_all[l:l + 1],
                          batch, tctx, seq)
        ym_l = _attention(q_l, (k_c, k_l), (v_c, v_l), batch, seq, min(seq, TQ_ATT), (tctx, seq))
        xs = _out_ffn(xs, yg_l, ym_l, wo, mod, None, n2, wg, wu, wd, fn, seq, TM_FFN, FFN_TILE, last, l)
        if not last:
            ym_c = _attention(q_c, (k_c,), (v_c,), batch, tctx, tctx, (tctx,))
            cs = _out_ffn(cs, yg_c, ym_c, wo, mod, ctx_row, n2, wg, wu, wd, fn, tctx, TM_FFN, FFN_TILE, False, l)
    return xs.reshape(batch, seq, d)
```
